```python
import jax, jax.numpy as jnp
from jax import lax
import numpy as np

D_MODEL = 2048
BATCH = 2
SEQ = 4096
DEPTH = 1

SB_HEADS = 16
SB_HEAD_DIM = 64
SB_WIDTH = SB_HEADS * SB_HEAD_DIM
Q_BLOCK = 128
SGU_GROUPS = 8
SGU_CHUNK = 128
SGU_WIDTH = 1024
SGU_GROUP_DIM = SGU_WIDTH // SGU_GROUPS
N_MEM = 256
XA_HEADS = 4
XA_HEAD_DIM = 128
XA_WIDTH = XA_HEADS * XA_HEAD_DIM
N_EXPERTS = 32
TOP_K = 4
D_FF = D_MODEL
SWIGLU_ALPHA = 1.702
SWIGLU_LIMIT = 7.0
RMS_EPS = 1e-5
IN_WIDTH = 3 * SB_WIDTH + 2 * SGU_WIDTH + 2 * D_MODEL
IN_SPLITS = [SB_WIDTH, 2 * SB_WIDTH, 3 * SB_WIDTH, 3 * SB_WIDTH + SGU_WIDTH,
             3 * SB_WIDTH + 2 * SGU_WIDTH, 3 * SB_WIDTH + 2 * SGU_WIDTH + D_MODEL]

kernel_name = "hybrid_stickbreak_sgu_memxattn_moe"


def rmsnorm(x, g):
    xf = x.astype(jnp.float32)
    r = lax.rsqrt(jnp.mean(xf * xf, axis=-1, keepdims=True) + RMS_EPS)
    return (xf * r).astype(x.dtype) * g


def stick_breaking_attention(q, k, v):
    B, H, S, Dh = q.shape
    nb = S // Q_BLOCK
    scale = Dh ** -0.5
    qb = q.reshape(B, H, nb, Q_BLOCK, Dh).transpose(2, 0, 1, 3, 4)
    k_pos = jnp.arange(S)

    def block(args):
        q_blk, i = args
        q_pos = i * Q_BLOCK + jnp.arange(Q_BLOCK)
        z = jnp.einsum('bhqd,bhkd->bhqk', q_blk, k,
                       preferred_element_type=jnp.float32) * scale
        causal = k_pos[None, :] < q_pos[:, None]
        log_beta = jax.nn.log_sigmoid(z)
        log_keep = jnp.where(causal, jax.nn.log_sigmoid(-z), 0.0)
        later = lax.cumsum(log_keep, axis=3, reverse=True) - log_keep
        a = jnp.where(causal, jnp.exp(log_beta + later), 0.0)
        return jnp.einsum('bhqk,bhkd->bhqd', a.astype(v.dtype), v)

    o = lax.map(block, (qb, jnp.arange(nb)))
    return o.transpose(1, 2, 0, 3, 4).reshape(B, H, S, Dh)


def chunked_spatial_gating(u, v, g_norm, w_s, b_s):
    B, S, _ = u.shape
    nc = S // SGU_CHUNK
    vg = rmsnorm(v.reshape(B, S, SGU_GROUPS, SGU_GROUP_DIM), g_norm)
    vg = vg.reshape(B, nc, SGU_CHUNK, SGU_GROUPS, SGU_GROUP_DIM)
    w = w_s * jnp.tril(jnp.ones((SGU_CHUNK, SGU_CHUNK), w_s.dtype))
    mixed = jnp.einsum('gpq,bcqgd->bcpgd', w, vg) + b_s.T[None, None, :, :, None]
    return u * mixed.reshape(B, S, SGU_WIDTH)


def memory_cross_attention(h, mem_h, w_q, w_kv, w_o):
    B, S, _ = h.shape
    M = mem_h.shape[1]
    q = (h @ w_q).reshape(B, S, XA_HEADS, XA_HEAD_DIM)
    k, v = jnp.split(mem_h @ w_kv, 2, axis=-1)
    k = k.reshape(B, M, XA_HEADS, XA_HEAD_DIM)
    v = v.reshape(B, M, XA_HEADS, XA_HEAD_DIM)
    s = jnp.einsum('bshd,bmhd->bhsm', q, k,
                   preferred_element_type=jnp.float32) * (XA_HEAD_DIM ** -0.5)
    p = jax.nn.softmax(s, axis=-1)
    o = jnp.einsum('bhsm,bmhd->bshd', p.astype(v.dtype), v).reshape(B, S, XA_WIDTH)
    return o @ w_o


def clamped_swiglu(a):
    glu, lin = a[..., ::2], a[..., 1::2]
    glu = jnp.minimum(glu, SWIGLU_LIMIT)
    lin = jnp.clip(lin, -SWIGLU_LIMIT, SWIGLU_LIMIT)
    return glu * jax.nn.sigmoid(SWIGLU_ALPHA * glu) * (lin + 1)


def moe_ffn(h, w_router, b_router, w_in, b_in, w_out, b_out):
    B, S, D = h.shape
    t = h.reshape(B * S, D)
    logits = (t @ w_router + b_router).astype(jnp.float32)
    top_vals, top_idx = lax.top_k(logits, TOP_K)
    top_w = jax.nn.softmax(top_vals, axis=-1)
    combine = jnp.sum(jax.nn.one_hot(top_idx, N_EXPERTS, dtype=jnp.float32)
                      * top_w[..., None], axis=1).astype(t.dtype)
    out = jnp.zeros_like(t)
    for e in range(N_EXPERTS):
        y = clamped_swiglu(t @ w_in[e] + b_in[e]) @ w_out[e] + b_out[e]
        out = out + combine[:, e:e + 1] * y
    return out.reshape(B, S, D)


def setup_inputs(seed: int = 0) -> dict:
    key = jax.random.key(seed)
    ks = jax.random.split(key, 24)
    L = DEPTH

    def nrm(k, shape, fan_in, gain=1.0):
        return gain * jax.random.normal(k, shape, jnp.float32) * (fan_in ** -0.5)

    def ones_noise(k, shape):
        return 1.0 + 0.02 * jax.random.normal(k, shape, jnp.float32)

    return {
        "x": jax.random.normal(ks[0], (BATCH, SEQ, D_MODEL), jnp.float32),
        "mem": jax.random.normal(ks[1], (BATCH, N_MEM, D_MODEL), jnp.float32),
        "norm_mix": ones_noise(ks[2], (L, D_MODEL)),
        "w_in": nrm(ks[3], (L, D_MODEL, IN_WIDTH), D_MODEL),
        "b_gate": 0.1 * jax.random.normal(ks[4], (L, 2 * D_MODEL), jnp.float32),
        "sgu_norm": ones_noise(ks[5], (L, SGU_GROUPS, SGU_GROUP_DIM)),
        "sgu_w": nrm(ks[6], (L, SGU_GROUPS, SGU_CHUNK, SGU_CHUNK), SGU_CHUNK, 0.5),
        "sgu_b": 1.0 + 0.1 * jax.random.normal(ks[7], (L, SGU_GROUPS, SGU_CHUNK), jnp.float32),
        "w_branch_a": nrm(ks[8], (L, SB_WIDTH, D_MODEL), SB_WIDTH),
        "w_branch_b": nrm(ks[9], (L, SGU_WIDTH, D_MODEL), SGU_WIDTH),
        "w_out": nrm(ks[10], (L, D_MODEL, D_MODEL), D_MODEL),
        "norm_xattn": ones_noise(ks[11], (L, D_MODEL)),
        "norm_mem": ones_noise(ks[12], (L, D_MODEL)),
        "xa_w_q": nrm(ks[13], (L, D_MODEL, XA_WIDTH), D_MODEL),
        "xa_w_kv": nrm(ks[14], (L, D_MODEL, 2 * XA_WIDTH), D_MODEL),
        "xa_w_o": nrm(ks[15], (L, XA_WIDTH, D_MODEL), XA_WIDTH),
        "norm_moe": ones_noise(ks[16], (L, D_MODEL)),
        "router_w": nrm(ks[17], (L, D_MODEL, N_EXPERTS), D_MODEL),
        "router_b": 0.01 * jax.random.normal(ks[18], (L, N_EXPERTS), jnp.float32),
        "moe_w_in": nrm(ks[19], (L, N_EXPERTS, D_MODEL, 2 * D_FF), D_MODEL),
        "moe_b_in": 0.02 * jax.random.normal(ks[20], (L, N_EXPERTS, 2 * D_FF), jnp.float32),
        "moe_w_out": nrm(ks[21], (L, N_EXPERTS, D_FF, D_MODEL), D_FF),
        "moe_b_out": 0.02 * jax.random.normal(ks[22], (L, N_EXPERTS, D_MODEL), jnp.float32),
        "norm_final": ones_noise(ks[23], (D_MODEL,)),
    }


def reference(x, mem, norm_mix, w_in, b_gate, sgu_norm, sgu_w, sgu_b, w_branch_a,
              w_branch_b, w_out, norm_xattn, norm_mem, xa_w_q, xa_w_kv, xa_w_o,
              norm_moe, router_w, router_b, moe_w_in, moe_b_in, moe_w_out, moe_b_out,
              norm_final):
    B, S, D = x.shape
    for l in range(DEPTH):
        h = rmsnorm(x, norm_mix[l])
        q, k, v, u_sgu, v_sgu, g_a, g_b = jnp.split(h @ w_in[l], IN_SPLITS, axis=-1)

        def heads(t):
            return t.reshape(B, S, SB_HEADS, SB_HEAD_DIM).transpose(0, 2, 1, 3)

        o_a = stick_breaking_attention(heads(q), heads(k), heads(v))
        o_a = o_a.transpose(0, 2, 1, 3).reshape(B, S, SB_WIDTH)
        o_b = chunked_spatial_gating(jax.nn.gelu(u_sgu, approximate=False),
                                     jax.nn.gelu(v_sgu, approximate=False),
                                     sgu_norm[l], sgu_w[l], sgu_b[l])
        gates = jax.nn.sigmoid(jnp.concatenate([g_a, g_b], axis=-1) + b_gate[l])
        gate_a, gate_b = jnp.split(gates, 2, axis=-1)
        merged = gate_a * (o_a @ w_branch_a[l]) + gate_b * (o_b @ w_branch_b[l])
        x = x + merged @ w_out[l]
        x = x + memory_cross_attention(rmsnorm(x, norm_xattn[l]), rmsnorm(mem, norm_mem[l]),
                                       xa_w_q[l], xa_w_kv[l], xa_w_o[l])
        x = x + moe_ffn(rmsnorm(x, norm_moe[l]), router_w[l], router_b[l],
                        moe_w_in[l], moe_b_in[l], moe_w_out[l], moe_b_out[l])
    return rmsnorm(x, norm_final)
```

```python
import functools

import jax
import jax.numpy as jnp
from jax import lax
from jax.experimental import pallas as pl
from jax.experimental.pallas import tpu as pltpu

SB_HEADS = 16
SB_HEAD_DIM = 64
SB_WIDTH = SB_HEADS * SB_HEAD_DIM
SGU_GROUPS = 8
SGU_CHUNK = 128
SGU_WIDTH = 1024
XA_HEADS = 4
XA_HEAD_DIM = 128
XA_WIDTH = XA_HEADS * XA_HEAD_DIM
N_EXPERTS = 32
TOP_K = 4
SWIGLU_ALPHA = 1.702
SWIGLU_LIMIT = 7.0
RMS_EPS = 1e-5

LANES = 128
COL_BLOCK = 1024
VMEM_LIMIT = 56 * 1024 * 1024

MOE_ROWS = 1024
MOE_SUB = 256
MOE_FC = 256

BF16 = jnp.bfloat16
F32 = jnp.float32


def _params(sem):
    return pltpu.CompilerParams(dimension_semantics=sem, vmem_limit_bytes=VMEM_LIMIT)


def _rms(x, g):
    r = lax.rsqrt(jnp.mean(x * x, axis=-1, keepdims=True) + RMS_EPS)
    return (x * r) * g


def _gelu(x):
    return 0.5 * x * (1.0 + lax.erf(x * (2.0 ** -0.5)))


def _rmsnorm_kernel(x_ref, g_ref, o_ref):
    o_ref[...] = _rms(x_ref[...], g_ref[...]).astype(o_ref.dtype)


def _rmsnorm(x, g, tm):
    t, d = x.shape
    return pl.pallas_call(
        _rmsnorm_kernel,
        grid=(t // tm,),
        in_specs=[pl.BlockSpec((tm, d), lambda i: (i, 0)),
                  pl.BlockSpec((1, d), lambda i: (0, 0))],
        out_specs=pl.BlockSpec((tm, d), lambda i: (i, 0)),
        out_shape=jax.ShapeDtypeStruct((t, d), BF16),
        compiler_params=_params(("parallel",)),
    )(x, g.reshape(1, d))


def _inproj_kernel(h_ref, w_ref, o_ref, wb_ref):
    @pl.when(pl.program_id(1) == 0)
    def _():
        wb_ref[...] = w_ref[...].astype(BF16)

    o_ref[...] = jnp.dot(h_ref[...], wb_ref[...],
                         preferred_element_type=F32).astype(o_ref.dtype)


def _inproj(h, w, tm, tn):
    t, d = h.shape
    n = w.shape[1]
    return pl.pallas_call(
        _inproj_kernel,
        grid=(n // tn, t // tm),
        in_specs=[pl.BlockSpec((tm, d), lambda j, i: (i, 0)),
                  pl.BlockSpec((d, tn), lambda j, i: (0, j))],
        out_specs=pl.BlockSpec((tm, tn), lambda j, i: (i, j)),
        out_shape=jax.ShapeDtypeStruct((t, n), BF16),
        scratch_shapes=[pltpu.VMEM((d, tn), BF16)],
        compiler_params=_params(("arbitrary", "arbitrary")),
    )(h, w)


def _sb_attn_kernel(q_ref, k_ref, v_ref, o_ref, *, tq):
    qi = pl.program_id(2)
    q2 = q_ref[...]
    lane = lax.broadcasted_iota(jnp.int32, (tq, LANES), 1)
    row = lax.broadcasted_iota(jnp.int32, (tq, tq), 0)
    col = lax.broadcasted_iota(jnp.int32, (tq, tq), 1)
    causal = col < row
    later = (row > col).astype(BF16)
    scale = SB_HEAD_DIM ** -0.5

    def block(qh, kb, carry, masked):
        c, acc = carry
        k0 = pl.multiple_of(kb * tq, tq)
        kblk = k_ref[pl.ds(k0, tq), :]
        vblk = v_ref[pl.ds(k0, tq), :]
        z = lax.dot_general(qh, kblk, (((1,), (1,)), ((), ())),
                            preferred_element_type=F32)
        sp = jnp.log(1.0 + jnp.exp(-jnp.abs(z)))
        mz = jnp.minimum(z, 0.0)
        log_beta = mz - sp
        log_keep = (mz - z) - sp
        if masked:
            log_keep = jnp.where(causal, log_keep, 0.0)
        hi = log_keep.astype(BF16)
        lo = (log_keep - hi.astype(F32)).astype(BF16)
        suffix = (jnp.dot(hi, later, preferred_element_type=F32)
                  + jnp.dot(lo, later, preferred_element_type=F32))
        a = jnp.exp(log_beta + c + suffix)
        if masked:
            a = jnp.where(causal, a, 0.0)
        acc = acc + jnp.dot(a.astype(BF16), vblk, preferred_element_type=F32)
        c = c + jnp.sum(log_keep, axis=-1, keepdims=True)
        return c, acc

    halves = []
    for half in range(2):
        in_half = (lane >= SB_HEAD_DIM * half) & (lane < SB_HEAD_DIM * (half + 1))
        qh = jnp.where(in_half, q2, jnp.zeros_like(q2)) * scale
        carry = (jnp.zeros((tq, 1), F32), jnp.zeros((tq, LANES), F32))
        carry = block(qh, qi, carry, True)
        carry = lax.fori_loop(
            0, qi, lambda t, cr: block(qh, qi - 1 - t, cr, False), carry)
        halves.append(carry[1])
    o_ref[...] = jnp.where(lane < SB_HEAD_DIM, halves[0], halves[1]).astype(o_ref.dtype)


def _sb_attention(proj, b, s, tq):
    pairs = SB_WIDTH // LANES
    nq = s // tq
    kcol = COL_BLOCK // LANES
    return pl.pallas_call(
        functools.partial(_sb_attn_kernel, tq=tq),
        grid=(b, pairs, nq),
        in_specs=[
            pl.BlockSpec((tq, LANES), lambda bi, p, i: (bi * nq + i, p)),
            pl.BlockSpec((s, LANES), lambda bi, p, i: (bi, kcol + p)),
            pl.BlockSpec((s, LANES), lambda bi, p, i: (bi, 2 * kcol + p)),
        ],
        out_specs=pl.BlockSpec((tq, LANES), lambda bi, p, i: (bi * nq + i, p)),
        out_shape=jax.ShapeDtypeStruct((b * s, SB_WIDTH), BF16),
        compiler_params=_params(("parallel", "parallel", "arbitrary")),
    )(proj, proj, proj)


def _mix_kernel(oa_ref, u_ref, vs_ref, ga0_ref, ga1_ref, gb0_ref, gb1_ref, x_ref,
                bg_ref, sn_ref, sw_ref, sbt_ref, wba_ref, wbb_ref, wo_ref,
                o_ref, ob_ref):
    tm = x_ref.shape[0]
    p_i = lax.broadcasted_iota(jnp.int32, (SGU_CHUNK, SGU_CHUNK), 0)
    q_i = lax.broadcasted_iota(jnp.int32, (SGU_CHUNK, SGU_CHUNK), 1)
    tril = q_i <= p_i
    for g in range(SGU_GROUPS):
        lanes = slice(g * LANES, (g + 1) * LANES)
        wg = jnp.where(tril, sw_ref[g], 0.0).astype(BF16)
        gn = sn_ref[g:g + 1, :]
        bcol = sbt_ref[:, g:g + 1]
        for c in range(tm // SGU_CHUNK):
            rows = slice(c * SGU_CHUNK, (c + 1) * SGU_CHUNK)
            vn = _rms(_gelu(vs_ref[rows, lanes].astype(F32)), gn)
            mixed = jnp.dot(wg, vn.astype(BF16), preferred_element_type=F32) + bcol
            uu = _gelu(u_ref[rows, lanes].astype(F32))
            ob_ref[rows, lanes] = (uu * mixed).astype(BF16)

    ya = jnp.dot(oa_ref[...], wba_ref[...], preferred_element_type=F32)
    yb = jnp.dot(ob_ref[...], wbb_ref[...], preferred_element_type=F32)
    half = ya.shape[1] // 2
    pieces = []
    for k, (ga_ref, gb_ref) in enumerate(((ga0_ref, gb0_ref), (ga1_ref, gb1_ref))):
        cols = slice(k * half, (k + 1) * half)
        ga = jax.nn.sigmoid(ga_ref[...].astype(F32) + bg_ref[0:1, cols])
        gb = jax.nn.sigmoid(gb_ref[...].astype(F32) + bg_ref[1:2, cols])
        pieces.append((ga * ya[:, cols] + gb * yb[:, cols]).astype(BF16))
    merged = jnp.concatenate(pieces, axis=1)
    o_ref[...] = x_ref[...] + jnp.dot(merged, wo_ref[...], preferred_element_type=F32)


def _const_spec(shape):
    nd = len(shape)
    return pl.BlockSpec(shape, lambda i: (0,) * nd, pipeline_mode=pl.Buffered(1))


def _mix(oa, proj, x, b_gate, sgu_norm, sgu_w, sgu_b, wba, wbb, wo, tm):
    t, d = x.shape

    def col(cidx):
        return pl.BlockSpec((tm, COL_BLOCK), lambda i: (i, cidx))

    return pl.pallas_call(
        _mix_kernel,
        grid=(t // tm,),
        in_specs=[
            pl.BlockSpec((tm, SB_WIDTH), lambda i: (i, 0)),
            col(3), col(4), col(5), col(6), col(7), col(8),
            pl.BlockSpec((tm, d), lambda i: (i, 0)),
            _const_spec((2, d)),
            _const_spec((SGU_GROUPS, LANES)),
            _const_spec((SGU_GROUPS, SGU_CHUNK, SGU_CHUNK)),
            _const_spec((SGU_CHUNK, SGU_GROUPS)),
            _const_spec((SB_WIDTH, d)),
            _const_spec((SGU_WIDTH, d)),
            _const_spec((d, d)),
        ],
        out_specs=pl.BlockSpec((tm, d), lambda i: (i, 0)),
        out_shape=jax.ShapeDtypeStruct((t, d), F32),
        scratch_shapes=[pltpu.VMEM((tm, SGU_WIDTH), BF16)],
        compiler_params=_params(("parallel",)),
    )(oa, proj, proj, proj, proj, proj, proj, x,
      b_gate.reshape(2, d), sgu_norm, sgu_w, sgu_b.T, wba, wbb, wo)


def _memkv_kernel(m_ref, g_ref, w_ref, o_ref):
    h = _rms(m_ref[...], g_ref[...]).astype(BF16)
    o_ref[...] = jnp.dot(h, w_ref[...].astype(BF16),
                         preferred_element_type=F32).astype(o_ref.dtype)


def _memkv(mem2d, g, w):
    m, d = mem2d.shape
    n = w.shape[1]
    tn = n // 2
    return pl.pallas_call(
        _memkv_kernel,
        grid=(n // tn,),
        in_specs=[pl.BlockSpec((m, d), lambda j: (0, 0)),
                  pl.BlockSpec((1, d), lambda j: (0, 0)),
                  pl.BlockSpec((d, tn), lambda j: (0, j))],
        out_specs=pl.BlockSpec((m, tn), lambda j: (0, j)),
        out_shape=jax.ShapeDtypeStruct((m, n), BF16),
        compiler_params=_params(("parallel",)),
    )(mem2d, g.reshape(1, d), w)


def _xattn_router_kernel(x1_ref, gx_ref, wq_ref, k_ref, v_ref, wo_ref, gm_ref,
                         wr_ref, br_ref, x2_ref, h3_ref, idx_ref, wt_ref, rank_ref,
                         cnt_ref):
    tm = x1_ref.shape[0]

    @pl.when(pl.program_id(0) == 0)
    def _():
        cnt_ref[...] = jnp.zeros_like(cnt_ref)

    x1 = x1_ref[...]
    hn = _rms(x1, gx_ref[...]).astype(BF16)
    q = jnp.dot(hn, wq_ref[...], preferred_element_type=F32)
    scale = XA_HEAD_DIM ** -0.5
    heads = []
    for h in range(XA_HEADS):
        cols = slice(h * XA_HEAD_DIM, (h + 1) * XA_HEAD_DIM)
        s = lax.dot_general(q[:, cols].astype(BF16), k_ref[0, :, cols],
                            (((1,), (1,)), ((), ())), preferred_element_type=F32) * scale
        e = jnp.exp(s - jnp.max(s, axis=-1, keepdims=True))
        p = e / jnp.sum(e, axis=-1, keepdims=True)
        heads.append(jnp.dot(p.astype(BF16), v_ref[0, :, cols],
                             preferred_element_type=F32).astype(BF16))
    o = jnp.concatenate(heads, axis=1)
    x2 = x1 + jnp.dot(o, wo_ref[...], preferred_element_type=F32)
    x2_ref[...] = x2

    h3 = _rms(x2, gm_ref[...])
    h3_ref[...] = h3.astype(BF16)
    logits = jnp.dot(h3, wr_ref[...], preferred_element_type=F32,
                     precision=lax.Precision.HIGHEST) + br_ref[...]

    lane = lax.broadcasted_iota(jnp.int32, (tm, LANES), 1)
    lane_f = lane.astype(F32)
    neg_inf = jnp.float32(-jnp.inf)
    work = logits
    hots, vals, ids = [], [], []
    for _ in range(TOP_K):
        m = jnp.max(work, axis=-1, keepdims=True)
        first = jnp.min(jnp.where(work == m, lane_f, float(LANES)), axis=-1, keepdims=True)
        hot = lane_f == first
        work = jnp.where(hot, neg_inf, work)
        hots.append(hot)
        vals.append(m)
        ids.append(first)

    exps = [jnp.exp(v - vals[0]) for v in vals]
    denom = exps[0] + exps[1] + exps[2] + exps[3]

    sel = jnp.zeros((tm, LANES), F32)
    for hot in hots:
        sel = sel + jnp.where(hot, 1.0, 0.0)
    r_i = lax.broadcasted_iota(jnp.int32, (tm, tm), 0)
    c_i = lax.broadcasted_iota(jnp.int32, (tm, tm), 1)
    before = (c_i < r_i).astype(BF16)
    seen = jnp.dot(before, sel.astype(BF16), preferred_element_type=F32) + cnt_ref[...]
    cnt_ref[...] = cnt_ref[...] + jnp.sum(sel, axis=0, keepdims=True)

    idx_out = jnp.zeros((tm, LANES), F32)
    wt_out = jnp.zeros((tm, LANES), F32)
    rank_out = jnp.zeros((tm, LANES), F32)
    for k in range(TOP_K):
        rank_k = jnp.sum(jnp.where(hots[k], seen, 0.0), axis=-1, keepdims=True)
        idx_out = jnp.where(lane == k, ids[k], idx_out)
        wt_out = jnp.where(lane == k, exps[k] / denom, wt_out)
        rank_out = jnp.where(lane == k, rank_k, rank_out)
    idx_ref[...] = idx_out.astype(jnp.int32)
    wt_ref[...] = wt_out
    rank_ref[...] = rank_out.astype(jnp.int32)


def _xattn_router(x1, kv, gx, wq, wo, gm, wr, br, b, tm):
    t, d = x1.shape
    per_b = t // b // tm
    n_mem = kv.shape[1]
    row = lambda i: (i, 0)
    return pl.pallas_call(
        _xattn_router_kernel,
        grid=(t // tm,),
        in_specs=[
            pl.BlockSpec((tm, d), row),
            _const_spec((1, d)),
            _const_spec((d, XA_WIDTH)),
            pl.BlockSpec((1, n_mem, XA_WIDTH), lambda i: (i // per_b, 0, 0)),
            pl.BlockSpec((1, n_mem, XA_WIDTH), lambda i: (i // per_b, 0, 1)),
            _const_spec((XA_WIDTH, d)),
            _const_spec((1, d)),
            _const_spec((d, LANES)),
            _const_spec((1, LANES)),
        ],
        out_specs=[
            pl.BlockSpec((tm, d), row),
            pl.BlockSpec((tm, d), row),
            pl.BlockSpec((tm, LANES), row),
            pl.BlockSpec((tm, LANES), row),
            pl.BlockSpec((tm, LANES), row),
        ],
        out_shape=[
            jax.ShapeDtypeStruct((t, d), F32),
            jax.ShapeDtypeStruct((t, d), BF16),
            jax.ShapeDtypeStruct((t, LANES), jnp.int32),
            jax.ShapeDtypeStruct((t, LANES), F32),
            jax.ShapeDtypeStruct((t, LANES), jnp.int32),
        ],
        scratch_shapes=[pltpu.VMEM((1, LANES), F32)],
        compiler_params=_params(("arbitrary",)),
    )(x1, gx.reshape(1, d), wq, kv, kv, wo, gm.reshape(1, d), wr, br)


def _moe_kernel(ie_ref, ns_ref, blk_ref, x_ref, rw_ref, w1_ref, b1_ref, w2_ref, b2_ref,
                o_ref, w1b_ref, w2f_ref, w2b_ref):
    del ie_ref, blk_ref
    i = pl.program_id(0)
    j = pl.program_id(1)
    nj = pl.num_programs(1)
    nsub = ns_ref[i]
    fc = w2_ref.shape[1]

    @pl.when(nsub > 0)
    def _():
        w1b_ref[...] = w1_ref[0].astype(BF16)
        for c in range(w2f_ref.shape[0]):
            cols = slice(c * LANES, (c + 1) * LANES)
            w2 = w2_ref[0, :, cols]
            w2f_ref[c, pl.ds(0, fc, stride=2), :] = w2
            w2f_ref[c, pl.ds(1, fc, stride=2), :] = w2
            w2b_ref[:, cols] = w2f_ref[c].astype(BF16)
        b1 = b1_ref[0]
        b2 = b2_ref[0]
        lane = lax.broadcasted_iota(jnp.int32, (MOE_SUB, 2 * fc), 1)
        even = (lane & 1) == 0

        def body(s, carry):
            rows = pl.ds(pl.multiple_of(s * MOE_SUB, MOE_SUB), MOE_SUB)
            a = jnp.dot(x_ref[rows, :], w1b_ref[...], preferred_element_type=F32) + b1
            glu = jnp.minimum(a, SWIGLU_LIMIT)
            act = glu * jax.nn.sigmoid(SWIGLU_ALPHA * glu)
            lin = jnp.clip(a, -SWIGLU_LIMIT, SWIGLU_LIMIT) + 1.0
            lin_on_glu = pltpu.roll(lin, 2 * fc - 1, 1)
            prod = jnp.where(even, act * lin_on_glu, 0.0).astype(BF16)
            y = jnp.dot(prod, w2b_ref[...], preferred_element_type=F32)

            @pl.when(j == 0)
            def _():
                o_ref[rows, :] = y + b2

            @pl.when((j > 0) & (j < nj - 1))
            def _():
                o_ref[rows, :] = o_ref[rows, :] + y

            @pl.when(j == nj - 1)
            def _():
                o_ref[rows, :] = (o_ref[rows, :] + y) * rw_ref[rows, :]

            return carry

        lax.fori_loop(0, nsub, body, 0)


def _moe(item_e, item_ns, item_blk, xs, roww, w_in, b_in, w_out, b_out):
    r_pad, d = xs.shape
    n_items = r_pad // MOE_ROWS
    e, _, ff2 = w_in.shape
    nj = ff2 // (2 * MOE_FC)

    def chunk(i, j, ns):
        return jnp.where(ns[i] > 0, j, nj - 1)

    grid_spec = pltpu.PrefetchScalarGridSpec(
        num_scalar_prefetch=3,
        grid=(n_items, nj),
        in_specs=[
            pl.BlockSpec((MOE_ROWS, d), lambda i, j, ie, ns, blk: (blk[i], 0)),
            pl.BlockSpec((MOE_ROWS, 1), lambda i, j, ie, ns, blk: (blk[i], 0)),
            pl.BlockSpec((1, d, 2 * MOE_FC),
                         lambda i, j, ie, ns, blk: (ie[i], 0, chunk(i, j, ns))),
            pl.BlockSpec((1, 1, 2 * MOE_FC),
                         lambda i, j, ie, ns, blk: (ie[i], 0, chunk(i, j, ns))),
            pl.BlockSpec((1, MOE_FC, d),
                         lambda i, j, ie, ns, blk: (ie[i], chunk(i, j, ns), 0)),
            pl.BlockSpec((1, 1, d), lambda i, j, ie, ns, blk: (ie[i], 0, 0)),
        ],
        out_specs=pl.BlockSpec((MOE_ROWS, d), lambda i, j, ie, ns, blk: (blk[i], 0)),
        scratch_shapes=[pltpu.VMEM((d, 2 * MOE_FC), BF16),
                        pltpu.VMEM((d // LANES, 2 * MOE_FC, LANES), F32),
                        pltpu.VMEM((2 * MOE_FC, d), BF16)],
    )
    return pl.pallas_call(
        _moe_kernel,
        grid_spec=grid_spec,
        out_shape=jax.ShapeDtypeStruct((r_pad, d), F32),
        compiler_params=_params(("arbitrary", "arbitrary")),
    )(item_e, item_ns, item_blk, xs, roww, w_in, b_in.reshape(e, 1, ff2), w_out,
      b_out.reshape(e, 1, d))


def _combine_kernel(x_ref, y_ref, g_ref, o_ref):
    d = x_ref.shape[1]
    acc = x_ref[...]
    for k in range(TOP_K):
        acc = acc + y_ref[:, k * d:(k + 1) * d]
    o_ref[...] = _rms(acc, g_ref[...])


def _combine(x2, yg, g, tm):
    t, d = x2.shape
    return pl.pallas_call(
        _combine_kernel,
        grid=(t // tm,),
        in_specs=[pl.BlockSpec((tm, d), lambda i: (i, 0)),
                  pl.BlockSpec((tm, TOP_K * d), lambda i: (i, 0)),
                  pl.BlockSpec((1, d), lambda i: (0, 0))],
        out_specs=pl.BlockSpec((tm, d), lambda i: (i, 0)),
        out_shape=jax.ShapeDtypeStruct((t, d), F32),
        compiler_params=_params(("parallel",)),
    )(x2, yg, g.reshape(1, d))


def _routing_tables(idx, rank, n_items):
    experts = jnp.arange(N_EXPERTS, dtype=jnp.int32)
    counts = jnp.sum((idx.reshape(-1, 1) == experts[None, :]).astype(jnp.int32), axis=0)
    n_it = (counts + MOE_ROWS - 1) // MOE_ROWS
    it_end = jnp.cumsum(n_it)
    it_start = it_end - n_it
    total = it_end[-1]
    dest = it_start[idx] * MOE_ROWS + rank

    items = jnp.arange(n_items, dtype=jnp.int32)
    valid = items < total
    last = total - 1
    item_blk = jnp.where(valid, items, last)
    item_e = jnp.searchsorted(it_end, item_blk, side="right").astype(jnp.int32)
    rows_left = counts[item_e] - (item_blk - it_start[item_e]) * MOE_ROWS
    n_sub = jnp.clip((rows_left + MOE_SUB - 1) // MOE_SUB, 0, MOE_ROWS // MOE_SUB)
    item_ns = jnp.where(valid, n_sub, 0).astype(jnp.int32)
    return dest, item_e, item_ns, item_blk.astype(jnp.int32)


def kernel(x, mem, norm_mix, w_in, b_gate, sgu_norm, sgu_w, sgu_b, w_branch_a, w_branch_b,
           w_out, norm_xattn, norm_mem, xa_w_q, xa_w_kv, xa_w_o, norm_moe, router_w,
           router_b, moe_w_in, moe_b_in, moe_w_out, moe_b_out, norm_final):
    b, s, d = x.shape
    t = b * s
    depth = norm_mix.shape[0]
    assert depth == 1, "the final rmsnorm is fused into the last layer's combine"
    n_mem = mem.shape[1]
    xt = x.reshape(t, d)
    tm_big = min(1024, t)
    n_items = (t * TOP_K) // MOE_ROWS + N_EXPERTS

    for l in range(depth):
        h = _rmsnorm(xt, norm_mix[l], 512)
        proj = _inproj(h, w_in[l], tm_big, COL_BLOCK)
        oa = _sb_attention(proj, b, s, 128)
        x1 = _mix(oa, proj, xt, b_gate[l], sgu_norm[l], sgu_w[l], sgu_b[l],
                  w_branch_a[l].astype(BF16), w_branch_b[l].astype(BF16),
                  w_out[l].astype(BF16), 256)

        kv = _memkv(mem.reshape(b * n_mem, d), norm_mem[l], xa_w_kv[l])
        kv = kv.reshape(b, n_mem, 2 * XA_WIDTH)
        wr = jnp.pad(router_w[l], ((0, 0), (0, LANES - N_EXPERTS)))
        br = jnp.pad(router_b[l], (0, LANES - N_EXPERTS),
                     constant_values=-jnp.inf).reshape(1, LANES)
        x2, h3, idx, wts, rank = _xattn_router(
            x1, kv, norm_xattn[l], xa_w_q[l].astype(BF16), xa_w_o[l].astype(BF16),
            norm_moe[l], wr, br, b, 256)
        idx, wts, rank = idx[:, :TOP_K], wts[:, :TOP_K], rank[:, :TOP_K]

        dest, item_e, item_ns, item_blk = _routing_tables(idx, rank, n_items)
        flat_dest = dest.reshape(-1)
        r_pad = n_items * MOE_ROWS
        src_tok = jnp.zeros((r_pad,), jnp.int32).at[flat_dest].set(
            jnp.arange(t * TOP_K, dtype=jnp.int32) // TOP_K)
        roww = jnp.zeros((r_pad,), F32).at[flat_dest].set(wts.reshape(-1))
        xs = jnp.take(h3, src_tok, axis=0)
        ys = _moe(item_e, item_ns, item_blk, xs, roww.reshape(r_pad, 1),
                  moe_w_in[l], moe_b_in[l], moe_w_out[l], moe_b_out[l])
        yg = jnp.take(ys, flat_dest, axis=0).reshape(t, TOP_K * d)
        xt = _combine(x2, yg, norm_final, 256)
    return xt.reshape(b, s, d)
```

```python
import functools

import jax
import jax.numpy as jnp
from jax import lax
from jax.experimental import pallas as pl
from jax.experimental.pallas import tpu as pltpu

SB_HEADS = 16
SB_HEAD_DIM = 64
SB_WIDTH = SB_HEADS * SB_HEAD_DIM
SGU_GROUPS = 8
SGU_CHUNK = 128
SGU_WIDTH = 1024
XA_HEADS = 4
XA_HEAD_DIM = 128
XA_WIDTH = XA_HEADS * XA_HEAD_DIM
N_EXPERTS = 32
TOP_K = 4
SWIGLU_ALPHA = 1.702
SWIGLU_LIMIT = 7.0
RMS_EPS = 1e-5

LANES = 128
MXU_DIM = 256
COL_BLOCK = 1024
VMEM_LIMIT = 56 * 1024 * 1024

ATT_TQ = 256
ATT_TK = 128

MOE_ROWS = 1280
MOE_SUB = 256
MOE_FC = 256

BF16 = jnp.bfloat16
F32 = jnp.float32


def _params(sem):
    return pltpu.CompilerParams(dimension_semantics=sem, vmem_limit_bytes=VMEM_LIMIT)


def _rms(x, g):
    r = lax.rsqrt(jnp.mean(x * x, axis=-1, keepdims=True) + RMS_EPS)
    return (x * r) * g


def _gelu(x):
    return 0.5 * x * (1.0 + lax.erf(x * (2.0 ** -0.5)))


def _rmsnorm_kernel(x_ref, g_ref, o_ref):
    o_ref[...] = _rms(x_ref[...], g_ref[...]).astype(o_ref.dtype)


def _rmsnorm(x, g, tm):
    t, d = x.shape
    return pl.pallas_call(
        _rmsnorm_kernel,
        grid=(t // tm,),
        in_specs=[pl.BlockSpec((tm, d), lambda i: (i, 0)),
                  pl.BlockSpec((1, d), lambda i: (0, 0))],
        out_specs=pl.BlockSpec((tm, d), lambda i: (i, 0)),
        out_shape=jax.ShapeDtypeStruct((t, d), BF16),
        compiler_params=_params(("parallel",)),
    )(x, g.reshape(1, d))


def _inproj_kernel(h_ref, w_ref, o_ref, wb_ref):
    @pl.when(pl.program_id(1) == 0)
    def _():
        wb_ref[...] = w_ref[...].astype(BF16)

    o_ref[...] = jnp.dot(h_ref[...], wb_ref[...],
                         preferred_element_type=F32).astype(o_ref.dtype)


def _inproj(h, w, tm, tn):
    t, d = h.shape
    n = w.shape[1]
    return pl.pallas_call(
        _inproj_kernel,
        grid=(n // tn, t // tm),
        in_specs=[pl.BlockSpec((tm, d), lambda j, i: (i, 0)),
                  pl.BlockSpec((d, tn), lambda j, i: (0, j))],
        out_specs=pl.BlockSpec((tm, tn), lambda j, i: (i, j)),
        out_shape=jax.ShapeDtypeStruct((t, n), BF16),
        scratch_shapes=[pltpu.VMEM((d, tn), BF16)],
        compiler_params=_params(("arbitrary", "arbitrary")),
    )(h, w)


def _sb_attn_kernel(q_ref, k_ref, v_ref, o_ref, c_ref, acc_ref):
    tq, tk = ATT_TQ, ATT_TK
    per = tq // tk
    qi = pl.program_id(2)
    lane = lax.broadcasted_iota(jnp.int32, (tq, LANES), 1)
    row = lax.broadcasted_iota(jnp.int32, (tq, tq), 0)
    col = lax.broadcasted_iota(jnp.int32, (tq, tq), 1)
    causal = col < row
    jj = lax.broadcasted_iota(jnp.int32, (2 * tk, 2 * tk), 0)
    ss = lax.broadcasted_iota(jnp.int32, (2 * tk, 2 * tk), 1)
    j_in = jnp.where(jj >= tk, jj - tk, jj)
    sums = jnp.where((ss >= tk) | (j_in > ss), 1.0, 0.0).astype(BF16)
    scale = SB_HEAD_DIM ** -0.5
    q2 = q_ref[...]
    qh = []
    for h in range(2):
        in_half = (lane >= SB_HEAD_DIM * h) & (lane < SB_HEAD_DIM * (h + 1))
        qh.append(jnp.where(in_half, q2, jnp.zeros_like(q2)) * scale)
    c_ref[...] = jnp.zeros_like(c_ref)
    acc_ref[...] = jnp.zeros_like(acc_ref)

    def step(kb, masked):
        k0 = pl.multiple_of(kb * tq, tq)
        kblk = k_ref[pl.ds(k0, tq), :]
        vblk = v_ref[pl.ds(k0, tq), :]
        zs = [lax.dot_general(qh[h], kblk, (((1,), (1,)), ((), ())),
                              preferred_element_type=F32) for h in range(2)]
        log_beta, log_keep = [], []
        for z in zs:
            sp = jnp.log(1.0 + jnp.exp(-jnp.abs(z)))
            mz = jnp.minimum(z, 0.0)
            lk = (mz - z) - sp
            if masked:
                lk = jnp.where(causal, lk, 0.0)
            log_beta.append(mz - sp)
            log_keep.append(lk)
        sfx = {}
        for h in range(2):
            for r in reversed(range(per)):
                part = log_keep[h][:, r * tk:(r + 1) * tk]
                hi = part.astype(BF16)
                lo = (part - hi.astype(F32)).astype(BF16)
                sfx[h, r] = jnp.dot(jnp.concatenate([hi, lo], axis=1), sums,
                                    preferred_element_type=F32)
        for h in range(2):
            c = c_ref[h]
            parts = [None] * per
            for r in reversed(range(per)):
                parts[r] = jnp.exp(log_beta[h][:, r * tk:(r + 1) * tk] + c + sfx[h, r][:, :tk])
                c = c + sfx[h, r][:, tk:]
            c_ref[h] = c
            a = jnp.concatenate(parts, axis=1)
            if masked:
                a = jnp.where(causal, a, 0.0)
            acc_ref[h] = acc_ref[h] + jnp.dot(a.astype(BF16), vblk,
                                              preferred_element_type=F32)

    step(qi, True)

    def body(t, carry):
        step(qi - 1 - t, False)
        return carry

    lax.fori_loop(0, qi, body, 0)
    o_ref[...] = jnp.where(lane < SB_HEAD_DIM, acc_ref[0], acc_ref[1]).astype(o_ref.dtype)


def _sb_attention(proj, b, s):
    tq = ATT_TQ
    pairs = SB_WIDTH // LANES
    nq = s // tq
    kcol = COL_BLOCK // LANES
    return pl.pallas_call(
        _sb_attn_kernel,
        grid=(b, pairs, nq),
        in_specs=[
            pl.BlockSpec((tq, LANES), lambda bi, p, i: (bi * nq + i, p)),
            pl.BlockSpec((s, LANES), lambda bi, p, i: (bi, kcol + p)),
            pl.BlockSpec((s, LANES), lambda bi, p, i: (bi, 2 * kcol + p)),
        ],
        out_specs=pl.BlockSpec((tq, LANES), lambda bi, p, i: (bi * nq + i, p)),
        out_shape=jax.ShapeDtypeStruct((b * s, SB_WIDTH), BF16),
        scratch_shapes=[pltpu.VMEM((2, tq, ATT_TK), F32),
                        pltpu.VMEM((2, tq, LANES), F32)],
        compiler_params=_params(("parallel", "parallel", "arbitrary")),
    )(proj, proj, proj)


def _mix_kernel(oa_ref, u_ref, vs_ref, ga0_ref, ga1_ref, gb0_ref, gb1_ref, x_ref,
                bg_ref, sn_ref, sw_ref, sbt_ref, wba_ref, wbb_ref, wo_ref,
                o_ref, ob_ref):
    tm = x_ref.shape[0]
    p_i = lax.broadcasted_iota(jnp.int32, (SGU_CHUNK, SGU_CHUNK), 0)
    q_i = lax.broadcasted_iota(jnp.int32, (SGU_CHUNK, SGU_CHUNK), 1)
    tril = q_i <= p_i
    for g in range(SGU_GROUPS):
        lanes = slice(g * LANES, (g + 1) * LANES)
        wg = jnp.where(tril, sw_ref[g], 0.0).astype(BF16)
        gn = sn_ref[g:g + 1, :]
        bcol = sbt_ref[:, g:g + 1]
        for c in range(tm // SGU_CHUNK):
            rows = slice(c * SGU_CHUNK, (c + 1) * SGU_CHUNK)
            vn = _rms(_gelu(vs_ref[rows, lanes].astype(F32)), gn)
            mixed = jnp.dot(wg, vn.astype(BF16), preferred_element_type=F32) + bcol
            uu = _gelu(u_ref[rows, lanes].astype(F32))
            ob_ref[rows, lanes] = (uu * mixed).astype(BF16)

    ya = jnp.dot(oa_ref[...], wba_ref[...], preferred_element_type=F32)
    yb = jnp.dot(ob_ref[...], wbb_ref[...], preferred_element_type=F32)
    half = ya.shape[1] // 2
    pieces = []
    for k, (ga_ref, gb_ref) in enumerate(((ga0_ref, gb0_ref), (ga1_ref, gb1_ref))):
        cols = slice(k * half, (k + 1) * half)
        ga = jax.nn.sigmoid(ga_ref[...].astype(F32) + bg_ref[0:1, cols])
        gb = jax.nn.sigmoid(gb_ref[...].astype(F32) + bg_ref[1:2, cols])
        pieces.append((ga * ya[:, cols] + gb * yb[:, cols]).astype(BF16))
    merged = jnp.concatenate(pieces, axis=1)
    o_ref[...] = x_ref[...] + jnp.dot(merged, wo_ref[...], preferred_element_type=F32)


def _const_spec(shape):
    nd = len(shape)
    return pl.BlockSpec(shape, lambda i: (0,) * nd, pipeline_mode=pl.Buffered(1))


def _mix(oa, proj, x, b_gate, sgu_norm, sgu_w, sgu_b, wba, wbb, wo, tm):
    t, d = x.shape

    def col(cidx):
        return pl.BlockSpec((tm, COL_BLOCK), lambda i: (i, cidx))

    return pl.pallas_call(
        _mix_kernel,
        grid=(t // tm,),
        in_specs=[
            pl.BlockSpec((tm, SB_WIDTH), lambda i: (i, 0)),
            col(3), col(4), col(5), col(6), col(7), col(8),
            pl.BlockSpec((tm, d), lambda i: (i, 0)),
            _const_spec((2, d)),
            _const_spec((SGU_GROUPS, LANES)),
            _const_spec((SGU_GROUPS, SGU_CHUNK, SGU_CHUNK)),
            _const_spec((SGU_CHUNK, SGU_GROUPS)),
            _const_spec((SB_WIDTH, d)),
            _const_spec((SGU_WIDTH, d)),
            _const_spec((d, d)),
        ],
        out_specs=pl.BlockSpec((tm, d), lambda i: (i, 0)),
        out_shape=jax.ShapeDtypeStruct((t, d), F32),
        scratch_shapes=[pltpu.VMEM((tm, SGU_WIDTH), BF16)],
        compiler_params=_params(("parallel",)),
    )(oa, proj, proj, proj, proj, proj, proj, x,
      b_gate.reshape(2, d), sgu_norm, sgu_w, sgu_b.T, wba, wbb, wo)


def _memkv_kernel(m_ref, g_ref, w_ref, o_ref):
    h = _rms(m_ref[...], g_ref[...]).astype(BF16)
    o_ref[...] = jnp.dot(h, w_ref[...].astype(BF16),
                         preferred_element_type=F32).astype(o_ref.dtype)


def _memkv(mem2d, g, w):
    m, d = mem2d.shape
    n = w.shape[1]
    tn = n // 2
    return pl.pallas_call(
        _memkv_kernel,
        grid=(n // tn,),
        in_specs=[pl.BlockSpec((m, d), lambda j: (0, 0)),
                  pl.BlockSpec((1, d), lambda j: (0, 0)),
                  pl.BlockSpec((d, tn), lambda j: (0, j))],
        out_specs=pl.BlockSpec((m, tn), lambda j: (0, j)),
        out_shape=jax.ShapeDtypeStruct((m, n), BF16),
        compiler_params=_params(("parallel",)),
    )(mem2d, g.reshape(1, d), w)


def _xattn_router_kernel(x1_ref, gx_ref, wq_ref, k_ref, v_ref, wo_ref, gm_ref,
                         wr_ref, br_ref, x2_ref, h3_ref, idx_ref, wt_ref, rank_ref,
                         cnt_ref):
    tm = x1_ref.shape[0]

    @pl.when(pl.program_id(0) == 0)
    def _():
        cnt_ref[...] = jnp.zeros_like(cnt_ref)

    x1 = x1_ref[...]
    hn = _rms(x1, gx_ref[...]).astype(BF16)
    q = jnp.dot(hn, wq_ref[...], preferred_element_type=F32)
    scale = XA_HEAD_DIM ** -0.5
    heads = []
    for h in range(XA_HEADS):
        cols = slice(h * XA_HEAD_DIM, (h + 1) * XA_HEAD_DIM)
        s = lax.dot_general(q[:, cols].astype(BF16), k_ref[0, :, cols],
                            (((1,), (1,)), ((), ())), preferred_element_type=F32) * scale
        e = jnp.exp(s - jnp.max(s, axis=-1, keepdims=True))
        p = e / jnp.sum(e, axis=-1, keepdims=True)
        heads.append(jnp.dot(p.astype(BF16), v_ref[0, :, cols],
                             preferred_element_type=F32).astype(BF16))
    o = jnp.concatenate(heads, axis=1)
    x2 = x1 + jnp.dot(o, wo_ref[...], preferred_element_type=F32)
    x2_ref[...] = x2

    h3 = _rms(x2, gm_ref[...])
    h3_ref[...] = h3.astype(BF16)
    logits = jnp.dot(h3, wr_ref[...], preferred_element_type=F32,
                     precision=lax.Precision.HIGHEST) + br_ref[...]

    lane = lax.broadcasted_iota(jnp.int32, (tm, LANES), 1)
    lane_f = lane.astype(F32)
    neg_inf = jnp.float32(-jnp.inf)
    work = logits
    hots, vals, ids = [], [], []
    for _ in range(TOP_K):
        m = jnp.max(work, axis=-1, keepdims=True)
        first = jnp.min(jnp.where(work == m, lane_f, float(LANES)), axis=-1, keepdims=True)
        hot = lane_f == first
        work = jnp.where(hot, neg_inf, work)
        hots.append(hot)
        vals.append(m)
        ids.append(first)

    exps = [jnp.exp(v - vals[0]) for v in vals]
    denom = exps[0] + exps[1] + exps[2] + exps[3]

    sel = jnp.zeros((tm, LANES), F32)
    for hot in hots:
        sel = sel + jnp.where(hot, 1.0, 0.0)
    r_i = lax.broadcasted_iota(jnp.int32, (tm, tm), 0)
    c_i = lax.broadcasted_iota(jnp.int32, (tm, tm), 1)
    before = (c_i < r_i).astype(BF16)
    seen = jnp.dot(before, sel.astype(BF16), preferred_element_type=F32) + cnt_ref[...]
    cnt_ref[...] = cnt_ref[...] + jnp.sum(sel, axis=0, keepdims=True)

    idx_out = jnp.zeros((tm, LANES), F32)
    wt_out = jnp.zeros((tm, LANES), F32)
    rank_out = jnp.zeros((tm, LANES), F32)
    for k in range(TOP_K):
        rank_k = jnp.sum(jnp.where(hots[k], seen, 0.0), axis=-1, keepdims=True)
        idx_out = jnp.where(lane == k, ids[k], idx_out)
        wt_out = jnp.where(lane == k, exps[k] / denom, wt_out)
        rank_out = jnp.where(lane == k, rank_k, rank_out)
    idx_ref[...] = idx_out.astype(jnp.int32)
    wt_ref[...] = wt_out
    rank_ref[...] = rank_out.astype(jnp.int32)


def _xattn_router(x1, kv, gx, wq, wo, gm, wr, br, b, tm):
    t, d = x1.shape
    per_b = t // b // tm
    n_mem = kv.shape[1]
    row = lambda i: (i, 0)
    return pl.pallas_call(
        _xattn_router_kernel,
        grid=(t // tm,),
        in_specs=[
            pl.BlockSpec((tm, d), row),
            _const_spec((1, d)),
            _const_spec((d, XA_WIDTH)),
            pl.BlockSpec((1, n_mem, XA_WIDTH), lambda i: (i // per_b, 0, 0)),
            pl.BlockSpec((1, n_mem, XA_WIDTH), lambda i: (i // per_b, 0, 1)),
            _const_spec((XA_WIDTH, d)),
            _const_spec((1, d)),
            _const_spec((d, LANES)),
            _const_spec((1, LANES)),
        ],
        out_specs=[
            pl.BlockSpec((tm, d), row),
            pl.BlockSpec((tm, d), row),
            pl.BlockSpec((tm, LANES), row),
            pl.BlockSpec((tm, LANES), row),
            pl.BlockSpec((tm, LANES), row),
        ],
        out_shape=[
            jax.ShapeDtypeStruct((t, d), F32),
            jax.ShapeDtypeStruct((t, d), BF16),
            jax.ShapeDtypeStruct((t, LANES), jnp.int32),
            jax.ShapeDtypeStruct((t, LANES), F32),
            jax.ShapeDtypeStruct((t, LANES), jnp.int32),
        ],
        scratch_shapes=[pltpu.VMEM((1, LANES), F32)],
        compiler_params=_params(("arbitrary",)),
    )(x1, gx.reshape(1, d), wq, kv, kv, wo, gm.reshape(1, d), wr, br)


def _swish_clamped(v):
    glu = jnp.minimum(v, SWIGLU_LIMIT)
    return glu * jax.nn.sigmoid(SWIGLU_ALPHA * glu)


def _lin_clamped(v):
    return jnp.clip(v, -SWIGLU_LIMIT, SWIGLU_LIMIT) + 1.0


def _moe_kernel(ie_ref, ns_ref, blk_ref, x_ref, w1_ref, b1_ref, w2_ref, b2_ref,
                o_ref, w1b_ref, w2b_ref):
    del ie_ref, blk_ref
    i = pl.program_id(0)
    j = pl.program_id(1)
    nsub = ns_ref[i]
    fc = w2_ref.shape[1]
    half = fc // 2
    d = o_ref.shape[1]

    @pl.when(nsub > 0)
    def _():
        w1b_ref[...] = w1_ref[0].astype(BF16)
        r_i = lax.broadcasted_iota(jnp.int32, (fc, fc), 0)
        c_i = lax.broadcasted_iota(jnp.int32, (fc, fc), 1)
        perm = jnp.where(c_i == (r_i >> 1) + half * (r_i & 1), 1.0, 0.0).astype(BF16)
        w2b_ref[...] = jnp.dot(perm, w2_ref[0].astype(BF16),
                               preferred_element_type=F32).astype(BF16)

        @pl.when(j == 0)
        def _():
            o_ref[...] = jnp.broadcast_to(b2_ref[0], o_ref.shape)

        b1 = b1_ref[0]
        lane = lax.broadcasted_iota(jnp.int32, (MOE_SUB, fc), 1)
        even = (lane & 1) == 0

        def sub_rows(s):
            return pl.ds(pl.multiple_of(s * MOE_SUB, MOE_SUB), MOE_SUB)

        def project_in(s):
            return jnp.dot(x_ref[sub_rows(s), :], w1b_ref[...],
                           preferred_element_type=F32) + b1

        def activate(a):
            a_a = a[:, :fc]
            a_b = a[:, fc:]
            p_a = _swish_clamped(a_a) * pltpu.roll(_lin_clamped(a_a), fc - 1, 1)
            p_b = pltpu.roll(_swish_clamped(a_b), 1, 1) * _lin_clamped(a_b)
            return jnp.where(even, p_a, p_b).astype(BF16)

        def project_out(s, act):
            rows = sub_rows(s)
            for n in range(d // MXU_DIM):
                cols = slice(n * MXU_DIM, (n + 1) * MXU_DIM)
                o_ref[rows, cols] = o_ref[rows, cols] + jnp.dot(
                    act, w2b_ref[:, cols], preferred_element_type=F32)

        def pair(p, carry):
            pre = [project_in(2 * p), project_in(2 * p + 1)]
            acts = [activate(a) for a in pre]
            project_out(2 * p, acts[0])
            project_out(2 * p + 1, acts[1])
            return carry

        lax.fori_loop(0, nsub // 2, pair, 0)

        @pl.when(nsub % 2 == 1)
        def _():
            project_out(nsub - 1, activate(project_in(nsub - 1)))


def _moe(item_e, item_ns, item_blk, xs, w_in, b_in, w_out, b_out):
    r_pad, d = xs.shape
    n_items = r_pad // MOE_ROWS
    e, _, ff2 = w_in.shape
    nj = ff2 // (2 * MOE_FC)

    def chunk(i, j, ns):
        return jnp.where(ns[i] > 0, j, nj - 1)

    grid_spec = pltpu.PrefetchScalarGridSpec(
        num_scalar_prefetch=3,
        grid=(n_items, nj),
        in_specs=[
            pl.BlockSpec((MOE_ROWS, d), lambda i, j, ie, ns, blk: (blk[i], 0)),
            pl.BlockSpec((1, d, 2 * MOE_FC),
                         lambda i, j, ie, ns, blk: (ie[i], 0, chunk(i, j, ns))),
            pl.BlockSpec((1, 1, 2 * MOE_FC),
                         lambda i, j, ie, ns, blk: (ie[i], 0, chunk(i, j, ns))),
            pl.BlockSpec((1, MOE_FC, d),
                         lambda i, j, ie, ns, blk: (ie[i], chunk(i, j, ns), 0)),
            pl.BlockSpec((1, 1, d), lambda i, j, ie, ns, blk: (ie[i], 0, 0)),
        ],
        out_specs=pl.BlockSpec((MOE_ROWS, d), lambda i, j, ie, ns, blk: (blk[i], 0)),
        scratch_shapes=[pltpu.VMEM((d, 2 * MOE_FC), BF16),
                        pltpu.VMEM((MOE_FC, d), BF16)],
    )
    return pl.pallas_call(
        _moe_kernel,
        grid_spec=grid_spec,
        out_shape=jax.ShapeDtypeStruct((r_pad, d), F32),
        compiler_params=_params(("arbitrary", "arbitrary")),
    )(item_e, item_ns, item_blk, xs, w_in, b_in.reshape(e, 1, ff2), w_out,
      b_out.reshape(e, 1, d))


def _combine_kernel(x_ref, y_ref, w_ref, g_ref, o_ref):
    d = x_ref.shape[1]
    acc = x_ref[...]
    w = w_ref[...]
    for k in range(TOP_K):
        acc = acc + w[:, k:k + 1] * y_ref[:, k * d:(k + 1) * d]
    o_ref[...] = _rms(acc, g_ref[...])


def _combine(x2, yg, wts, g, tm):
    t, d = x2.shape
    return pl.pallas_call(
        _combine_kernel,
        grid=(t // tm,),
        in_specs=[pl.BlockSpec((tm, d), lambda i: (i, 0)),
                  pl.BlockSpec((tm, TOP_K * d), lambda i: (i, 0)),
                  pl.BlockSpec((tm, LANES), lambda i: (i, 0)),
                  pl.BlockSpec((1, d), lambda i: (0, 0))],
        out_specs=pl.BlockSpec((tm, d), lambda i: (i, 0)),
        out_shape=jax.ShapeDtypeStruct((t, d), F32),
        compiler_params=_params(("parallel",)),
    )(x2, yg, wts, g.reshape(1, d))


def _routing_tables(idx, rank, n_items):
    experts = jnp.arange(N_EXPERTS, dtype=jnp.int32)
    counts = jnp.sum((idx.reshape(-1, 1) == experts[None, :]).astype(jnp.int32), axis=0)
    n_it = (counts + MOE_ROWS - 1) // MOE_ROWS
    it_end = jnp.cumsum(n_it)
    it_start = it_end - n_it
    total = it_end[-1]
    dest = it_start[idx] * MOE_ROWS + rank

    items = jnp.arange(n_items, dtype=jnp.int32)
    valid = items < total
    item_blk = jnp.where(valid, items, total - 1)
    item_e = jnp.sum((it_end[None, :] <= item_blk[:, None]).astype(jnp.int32), axis=1)
    rows_left = counts[item_e] - (item_blk - it_start[item_e]) * MOE_ROWS
    n_sub = jnp.clip((rows_left + MOE_SUB - 1) // MOE_SUB, 0, MOE_ROWS // MOE_SUB)
    item_ns = jnp.where(valid, n_sub, 0).astype(jnp.int32)
    return dest, item_e, item_ns, item_blk.astype(jnp.int32)


def kernel(x, mem, norm_mix, w_in, b_gate, sgu_norm, sgu_w, sgu_b, w_branch_a, w_branch_b,
           w_out, norm_xattn, norm_mem, xa_w_q, xa_w_kv, xa_w_o, norm_moe, router_w,
           router_b, moe_w_in, moe_b_in, moe_w_out, moe_b_out, norm_final):
    b, s, d = x.shape
    t = b * s
    depth = norm_mix.shape[0]
    assert depth == 1, "the final rmsnorm is fused into the last layer's combine"
    n_mem = mem.shape[1]
    xt = x.reshape(t, d)
    tm_big = min(1024, t)
    n_items = -(-(t * TOP_K) // MOE_ROWS) + N_EXPERTS

    for l in range(depth):
        h = _rmsnorm(xt, norm_mix[l], 512)
        proj = _inproj(h, w_in[l], tm_big, COL_BLOCK)
        oa = _sb_attention(proj, b, s)
        x1 = _mix(oa, proj, xt, b_gate[l], sgu_norm[l], sgu_w[l], sgu_b[l],
                  w_branch_a[l].astype(BF16), w_branch_b[l].astype(BF16),
                  w_out[l].astype(BF16), 256)

        kv = _memkv(mem.reshape(b * n_mem, d), norm_mem[l], xa_w_kv[l])
        kv = kv.reshape(b, n_mem, 2 * XA_WIDTH)
        wr = jnp.pad(router_w[l], ((0, 0), (0, LANES - N_EXPERTS)))
        br = jnp.pad(router_b[l], (0, LANES - N_EXPERTS),
                     constant_values=-jnp.inf).reshape(1, LANES)
        x2, h3, idx, wts, rank = _xattn_router(
            x1, kv, norm_xattn[l], xa_w_q[l].astype(BF16), xa_w_o[l].astype(BF16),
            norm_moe[l], wr, br, b, 256)

        dest, item_e, item_ns, item_blk = _routing_tables(
            idx[:, :TOP_K], rank[:, :TOP_K], n_items)
        flat_dest = dest.reshape(-1)
        r_pad = n_items * MOE_ROWS
        src_tok = jnp.zeros((r_pad,), jnp.int32).at[flat_dest].set(
            jnp.arange(t * TOP_K, dtype=jnp.int32) // TOP_K)
        xs = jnp.take(h3, src_tok, axis=0)
        ys = _moe(item_e, item_ns, item_blk, xs,
                  moe_w_in[l], moe_b_in[l], moe_w_out[l], moe_b_out[l])
        yg = jnp.take(ys, flat_dest, axis=0).reshape(t, TOP_K * d)
        xt = _combine(x2, yg, wts, norm_final, 256)
    return xt.reshape(b, s, d)
```

```python
import functools

import jax
import jax.numpy as jnp
from jax import lax
from jax.experimental import pallas as pl
from jax.experimental.pallas import tpu as pltpu

SB_HEADS = 16
SB_HEAD_DIM = 64
SB_WIDTH = SB_HEADS * SB_HEAD_DIM
SGU_GROUPS = 8
SGU_CHUNK = 128
SGU_WIDTH = 1024
XA_HEADS = 4
XA_HEAD_DIM = 128
XA_WIDTH = XA_HEADS * XA_HEAD_DIM
N_EXPERTS = 32
TOP_K = 4
SWIGLU_ALPHA = 1.702
SWIGLU_LIMIT = 7.0
RMS_EPS = 1e-5

LANES = 128
MXU_DIM = 256
COL_BLOCK = 1024
VMEM_LIMIT = 56 * 1024 * 1024

ATT_TQ = 256
ATT_TK = 128

MOE_ROWS = 1280
MOE_SUB = 256
MOE_FC = 256

BF16 = jnp.bfloat16
F32 = jnp.float32


def _params(sem):
    return pltpu.CompilerParams(dimension_semantics=sem, vmem_limit_bytes=VMEM_LIMIT)


def _rms(x, g):
    r = lax.rsqrt(jnp.mean(x * x, axis=-1, keepdims=True) + RMS_EPS)
    return (x * r) * g


def _gelu(x):
    return 0.5 * x * (1.0 + lax.erf(x * (2.0 ** -0.5)))


def _rmsnorm_kernel(x_ref, g_ref, o_ref):
    o_ref[...] = _rms(x_ref[...], g_ref[...]).astype(o_ref.dtype)


def _rmsnorm(x, g, tm):
    t, d = x.shape
    return pl.pallas_call(
        _rmsnorm_kernel,
        grid=(t // tm,),
        in_specs=[pl.BlockSpec((tm, d), lambda i: (i, 0)),
                  pl.BlockSpec((1, d), lambda i: (0, 0))],
        out_specs=pl.BlockSpec((tm, d), lambda i: (i, 0)),
        out_shape=jax.ShapeDtypeStruct((t, d), BF16),
        compiler_params=_params(("parallel",)),
    )(x, g.reshape(1, d))


def _inproj_kernel(h_ref, w_ref, o_ref, wb_ref):
    @pl.when(pl.program_id(1) == 0)
    def _():
        wb_ref[...] = w_ref[...].astype(BF16)

    o_ref[...] = jnp.dot(h_ref[...], wb_ref[...],
                         preferred_element_type=F32).astype(o_ref.dtype)


def _inproj(h, w, tm, tn):
    t, d = h.shape
    n = w.shape[1]
    return pl.pallas_call(
        _inproj_kernel,
        grid=(n // tn, t // tm),
        in_specs=[pl.BlockSpec((tm, d), lambda j, i: (i, 0)),
                  pl.BlockSpec((d, tn), lambda j, i: (0, j))],
        out_specs=pl.BlockSpec((tm, tn), lambda j, i: (i, j)),
        out_shape=jax.ShapeDtypeStruct((t, n), BF16),
        scratch_shapes=[pltpu.VMEM((d, tn), BF16)],
        compiler_params=_params(("arbitrary", "arbitrary")),
    )(h, w)


def _sb_attn_kernel(q_ref, k_ref, v_ref, o_ref, c_ref, acc_ref):
    tq, tk = ATT_TQ, ATT_TK
    per = tq // tk
    qi = pl.program_id(2)
    lane = lax.broadcasted_iota(jnp.int32, (tq, LANES), 1)
    row = lax.broadcasted_iota(jnp.int32, (tq, tq), 0)
    col = lax.broadcasted_iota(jnp.int32, (tq, tq), 1)
    causal = col < row
    jj = lax.broadcasted_iota(jnp.int32, (2 * tk, 2 * tk), 0)
    ss = lax.broadcasted_iota(jnp.int32, (2 * tk, 2 * tk), 1)
    j_in = jnp.where(jj >= tk, jj - tk, jj)
    sums = jnp.where((ss >= tk) | (j_in > ss), 1.0, 0.0).astype(BF16)
    scale = SB_HEAD_DIM ** -0.5
    q2 = q_ref[...]
    qh = []
    for h in range(2):
        in_half = (lane >= SB_HEAD_DIM * h) & (lane < SB_HEAD_DIM * (h + 1))
        qh.append(jnp.where(in_half, q2, jnp.zeros_like(q2)) * scale)
    c_ref[...] = jnp.zeros_like(c_ref)
    acc_ref[...] = jnp.zeros_like(acc_ref)

    def step(kb, masked):
        k0 = pl.multiple_of(kb * tq, tq)
        kblk = k_ref[pl.ds(k0, tq), :]
        vblk = v_ref[pl.ds(k0, tq), :]
        zs = [lax.dot_general(qh[h], kblk, (((1,), (1,)), ((), ())),
                              preferred_element_type=F32) for h in range(2)]
        log_beta, log_keep = [], []
        for z in zs:
            sp = jnp.log(1.0 + jnp.exp(-jnp.abs(z)))
            mz = jnp.minimum(z, 0.0)
            lk = (mz - z) - sp
            if masked:
                lk = jnp.where(causal, lk, 0.0)
            log_beta.append(mz - sp)
            log_keep.append(lk)
        sfx = {}
        for h in range(2):
            for r in reversed(range(per)):
                part = log_keep[h][:, r * tk:(r + 1) * tk]
                hi = part.astype(BF16)
                lo = (part - hi.astype(F32)).astype(BF16)
                sfx[h, r] = jnp.dot(jnp.concatenate([hi, lo], axis=1), sums,
                                    preferred_element_type=F32)
        for h in range(2):
            c = c_ref[h]
            parts = [None] * per
            for r in reversed(range(per)):
                parts[r] = jnp.exp(log_beta[h][:, r * tk:(r + 1) * tk] + c + sfx[h, r][:, :tk])
                c = c + sfx[h, r][:, tk:]
            c_ref[h] = c
            a = jnp.concatenate(parts, axis=1)
            if masked:
                a = jnp.where(causal, a, 0.0)
            acc_ref[h] = acc_ref[h] + jnp.dot(a.astype(BF16), vblk,
                                              preferred_element_type=F32)

    step(qi, True)

    def body(t, carry):
        step(qi - 1 - t, False)
        return carry

    lax.fori_loop(0, qi, body, 0)
    o_ref[...] = jnp.where(lane < SB_HEAD_DIM, acc_ref[0], acc_ref[1]).astype(o_ref.dtype)


def _sb_attention(proj, b, s):
    tq = ATT_TQ
    pairs = SB_WIDTH // LANES
    nq = s // tq
    kcol = COL_BLOCK // LANES
    return pl.pallas_call(
        _sb_attn_kernel,
        grid=(b, pairs, nq),
        in_specs=[
            pl.BlockSpec((tq, LANES), lambda bi, p, i: (bi * nq + i, p)),
            pl.BlockSpec((s, LANES), lambda bi, p, i: (bi, kcol + p)),
            pl.BlockSpec((s, LANES), lambda bi, p, i: (bi, 2 * kcol + p)),
        ],
        out_specs=pl.BlockSpec((tq, LANES), lambda bi, p, i: (bi * nq + i, p)),
        out_shape=jax.ShapeDtypeStruct((b * s, SB_WIDTH), BF16),
        scratch_shapes=[pltpu.VMEM((2, tq, ATT_TK), F32),
                        pltpu.VMEM((2, tq, LANES), F32)],
        compiler_params=_params(("parallel", "parallel", "arbitrary")),
    )(proj, proj, proj)


def _mix_kernel(oa_ref, u_ref, vs_ref, ga0_ref, ga1_ref, gb0_ref, gb1_ref, x_ref,
                bg_ref, sn_ref, sw_ref, sbt_ref, wba_ref, wbb_ref, wo_ref,
                o_ref, ob_ref):
    tm = x_ref.shape[0]
    p_i = lax.broadcasted_iota(jnp.int32, (SGU_CHUNK, SGU_CHUNK), 0)
    q_i = lax.broadcasted_iota(jnp.int32, (SGU_CHUNK, SGU_CHUNK), 1)
    tril = q_i <= p_i
    for g in range(SGU_GROUPS):
        lanes = slice(g * LANES, (g + 1) * LANES)
        wg = jnp.where(tril, sw_ref[g], 0.0).astype(BF16)
        gn = sn_ref[g:g + 1, :]
        bcol = sbt_ref[:, g:g + 1]
        for c in range(tm // SGU_CHUNK):
            rows = slice(c * SGU_CHUNK, (c + 1) * SGU_CHUNK)
            vn = _rms(_gelu(vs_ref[rows, lanes].astype(F32)), gn)
            mixed = jnp.dot(wg, vn.astype(BF16), preferred_element_type=F32) + bcol
            uu = _gelu(u_ref[rows, lanes].astype(F32))
            ob_ref[rows, lanes] = (uu * mixed).astype(BF16)

    ya = jnp.dot(oa_ref[...], wba_ref[...], preferred_element_type=F32)
    yb = jnp.dot(ob_ref[...], wbb_ref[...], preferred_element_type=F32)
    half = ya.shape[1] // 2
    pieces = []
    for k, (ga_ref, gb_ref) in enumerate(((ga0_ref, gb0_ref), (ga1_ref, gb1_ref))):
        cols = slice(k * half, (k + 1) * half)
        ga = jax.nn.sigmoid(ga_ref[...].astype(F32) + bg_ref[0:1, cols])
        gb = jax.nn.sigmoid(gb_ref[...].astype(F32) + bg_ref[1:2, cols])
        pieces.append((ga * ya[:, cols] + gb * yb[:, cols]).astype(BF16))
    merged = jnp.concatenate(pieces, axis=1)
    o_ref[...] = x_ref[...] + jnp.dot(merged, wo_ref[...], preferred_element_type=F32)


def _const_spec(shape):
    nd = len(shape)
    return pl.BlockSpec(shape, lambda i: (0,) * nd, pipeline_mode=pl.Buffered(1))


def _mix(oa, proj, x, b_gate, sgu_norm, sgu_w, sgu_b, wba, wbb, wo, tm):
    t, d = x.shape

    def col(cidx):
        return pl.BlockSpec((tm, COL_BLOCK), lambda i: (i, cidx))

    return pl.pallas_call(
        _mix_kernel,
        grid=(t // tm,),
        in_specs=[
            pl.BlockSpec((tm, SB_WIDTH), lambda i: (i, 0)),
            col(3), col(4), col(5), col(6), col(7), col(8),
            pl.BlockSpec((tm, d), lambda i: (i, 0)),
            _const_spec((2, d)),
            _const_spec((SGU_GROUPS, LANES)),
            _const_spec((SGU_GROUPS, SGU_CHUNK, SGU_CHUNK)),
            _const_spec((SGU_CHUNK, SGU_GROUPS)),
            _const_spec((SB_WIDTH, d)),
            _const_spec((SGU_WIDTH, d)),
            _const_spec((d, d)),
        ],
        out_specs=pl.BlockSpec((tm, d), lambda i: (i, 0)),
        out_shape=jax.ShapeDtypeStruct((t, d), F32),
        scratch_shapes=[pltpu.VMEM((tm, SGU_WIDTH), BF16)],
        compiler_params=_params(("parallel",)),
    )(oa, proj, proj, proj, proj, proj, proj, x,
      b_gate.reshape(2, d), sgu_norm, sgu_w, sgu_b.T, wba, wbb, wo)


def _memkv_kernel(m_ref, g_ref, w_ref, o_ref):
    h = _rms(m_ref[...], g_ref[...]).astype(BF16)
    o_ref[...] = jnp.dot(h, w_ref[...].astype(BF16),
                         preferred_element_type=F32).astype(o_ref.dtype)


def _memkv(mem2d, g, w):
    m, d = mem2d.shape
    n = w.shape[1]
    tn = n // 2
    return pl.pallas_call(
        _memkv_kernel,
        grid=(n // tn,),
        in_specs=[pl.BlockSpec((m, d), lambda j: (0, 0)),
                  pl.BlockSpec((1, d), lambda j: (0, 0)),
                  pl.BlockSpec((d, tn), lambda j: (0, j))],
        out_specs=pl.BlockSpec((m, tn), lambda j: (0, j)),
        out_shape=jax.ShapeDtypeStruct((m, n), BF16),
        compiler_params=_params(("parallel",)),
    )(mem2d, g.reshape(1, d), w)


def _xattn_router_kernel(x1_ref, gx_ref, wq_ref, k_ref, v_ref, wo_ref, gm_ref,
                         wr_ref, br_ref, x2_ref, h3_ref, idx_ref, wt_ref, rank_ref,
                         cnt_ref):
    tm = x1_ref.shape[0]

    @pl.when(pl.program_id(0) == 0)
    def _():
        cnt_ref[...] = jnp.zeros_like(cnt_ref)

    x1 = x1_ref[...]
    hn = _rms(x1, gx_ref[...]).astype(BF16)
    q = jnp.dot(hn, wq_ref[...], preferred_element_type=F32)
    scale = XA_HEAD_DIM ** -0.5
    heads = []
    for h in range(XA_HEADS):
        cols = slice(h * XA_HEAD_DIM, (h + 1) * XA_HEAD_DIM)
        s = lax.dot_general(q[:, cols].astype(BF16), k_ref[0, :, cols],
                            (((1,), (1,)), ((), ())), preferred_element_type=F32) * scale
        e = jnp.exp(s - jnp.max(s, axis=-1, keepdims=True))
        p = e / jnp.sum(e, axis=-1, keepdims=True)
        heads.append(jnp.dot(p.astype(BF16), v_ref[0, :, cols],
                             preferred_element_type=F32).astype(BF16))
    o = jnp.concatenate(heads, axis=1)
    x2 = x1 + jnp.dot(o, wo_ref[...], preferred_element_type=F32)
    x2_ref[...] = x2

    h3 = _rms(x2, gm_ref[...])
    h3_ref[...] = h3.astype(BF16)
    logits = jnp.dot(h3, wr_ref[...], preferred_element_type=F32,
                     precision=lax.Precision.HIGHEST) + br_ref[...]

    lane = lax.broadcasted_iota(jnp.int32, (tm, LANES), 1)
    lane_f = lane.astype(F32)
    neg_inf = jnp.float32(-jnp.inf)
    work = logits
    hots, vals, ids = [], [], []
    for _ in range(TOP_K):
        m = jnp.max(work, axis=-1, keepdims=True)
        first = jnp.min(jnp.where(work == m, lane_f, float(LANES)), axis=-1, keepdims=True)
        hot = lane_f == first
        work = jnp.where(hot, neg_inf, work)
        hots.append(hot)
        vals.append(m)
        ids.append(first)

    exps = [jnp.exp(v - vals[0]) for v in vals]
    denom = exps[0] + exps[1] + exps[2] + exps[3]

    sel = jnp.zeros((tm, LANES), F32)
    for hot in hots:
        sel = sel + jnp.where(hot, 1.0, 0.0)
    r_i = lax.broadcasted_iota(jnp.int32, (tm, tm), 0)
    c_i = lax.broadcasted_iota(jnp.int32, (tm, tm), 1)
    before = (c_i < r_i).astype(BF16)
    seen = jnp.dot(before, sel.astype(BF16), preferred_element_type=F32) + cnt_ref[...]
    cnt_ref[...] = cnt_ref[...] + jnp.sum(sel, axis=0, keepdims=True)

    idx_out = jnp.zeros((tm, LANES), F32)
    wt_out = jnp.zeros((tm, LANES), F32)
    rank_out = jnp.zeros((tm, LANES), F32)
    for k in range(TOP_K):
        rank_k = jnp.sum(jnp.where(hots[k], seen, 0.0), axis=-1, keepdims=True)
        idx_out = jnp.where(lane == k, ids[k], idx_out)
        wt_out = jnp.where(lane == k, exps[k] / denom, wt_out)
        rank_out = jnp.where(lane == k, rank_k, rank_out)
    idx_ref[...] = idx_out.astype(jnp.int32)
    wt_ref[...] = wt_out
    rank_ref[...] = rank_out.astype(jnp.int32)


def _xattn_router(x1, kv, gx, wq, wo, gm, wr, br, b, tm):
    t, d = x1.shape
    per_b = t // b // tm
    n_mem = kv.shape[1]
    row = lambda i: (i, 0)
    return pl.pallas_call(
        _xattn_router_kernel,
        grid=(t // tm,),
        in_specs=[
            pl.BlockSpec((tm, d), row),
            _const_spec((1, d)),
            _const_spec((d, XA_WIDTH)),
            pl.BlockSpec((1, n_mem, XA_WIDTH), lambda i: (i // per_b, 0, 0)),
            pl.BlockSpec((1, n_mem, XA_WIDTH), lambda i: (i // per_b, 0, 1)),
            _const_spec((XA_WIDTH, d)),
            _const_spec((1, d)),
            _const_spec((d, LANES)),
            _const_spec((1, LANES)),
        ],
        out_specs=[
            pl.BlockSpec((tm, d), row),
            pl.BlockSpec((tm, d), row),
            pl.BlockSpec((tm, LANES), row),
            pl.BlockSpec((tm, LANES), row),
            pl.BlockSpec((tm, LANES), row),
        ],
        out_shape=[
            jax.ShapeDtypeStruct((t, d), F32),
            jax.ShapeDtypeStruct((t, d), BF16),
            jax.ShapeDtypeStruct((t, LANES), jnp.int32),
            jax.ShapeDtypeStruct((t, LANES), F32),
            jax.ShapeDtypeStruct((t, LANES), jnp.int32),
        ],
        scratch_shapes=[pltpu.VMEM((1, LANES), F32)],
        compiler_params=_params(("arbitrary",)),
    )(x1, gx.reshape(1, d), wq, kv, kv, wo, gm.reshape(1, d), wr, br)


def _swish_clamped(v):
    glu = jnp.minimum(v, SWIGLU_LIMIT)
    return glu * jax.nn.sigmoid(SWIGLU_ALPHA * glu)


def _lin_clamped(v):
    return jnp.clip(v, -SWIGLU_LIMIT, SWIGLU_LIMIT) + 1.0


def _moe_kernel(ie_ref, ns_ref, blk_ref, x_ref, w1_ref, b1_ref, w2_ref, b2_ref,
                o_ref, w1b_ref, w2b_ref):
    del ie_ref, blk_ref
    i = pl.program_id(0)
    j = pl.program_id(1)
    nsub = ns_ref[i]
    fc = w2_ref.shape[1]
    half = fc // 2
    d = o_ref.shape[1]

    @pl.when(nsub > 0)
    def _():
        w1b_ref[...] = w1_ref[0].astype(BF16)
        r_i = lax.broadcasted_iota(jnp.int32, (fc, fc), 0)
        c_i = lax.broadcasted_iota(jnp.int32, (fc, fc), 1)
        perm = jnp.where(c_i == (r_i >> 1) + half * (r_i & 1), 1.0, 0.0).astype(BF16)
        w2b_ref[...] = jnp.dot(perm, w2_ref[0].astype(BF16),
                               preferred_element_type=F32).astype(BF16)

        @pl.when(j == 0)
        def _():
            o_ref[...] = jnp.broadcast_to(b2_ref[0], o_ref.shape)

        b1 = b1_ref[0]
        lane = lax.broadcasted_iota(jnp.int32, (MOE_SUB, fc), 1)
        even = (lane & 1) == 0

        def sub_rows(s):
            return pl.ds(pl.multiple_of(s * MOE_SUB, MOE_SUB), MOE_SUB)

        def project_in(s):
            return jnp.dot(x_ref[sub_rows(s), :], w1b_ref[...],
                           preferred_element_type=F32) + b1

        def activate(a):
            a_a = a[:, :fc]
            a_b = a[:, fc:]
            p_a = _swish_clamped(a_a) * pltpu.roll(_lin_clamped(a_a), fc - 1, 1)
            p_b = pltpu.roll(_swish_clamped(a_b), 1, 1) * _lin_clamped(a_b)
            return jnp.where(even, p_a, p_b).astype(BF16)

        def project_out(s, act):
            rows = sub_rows(s)
            for n in range(d // MXU_DIM):
                cols = slice(n * MXU_DIM, (n + 1) * MXU_DIM)
                o_ref[rows, cols] = o_ref[rows, cols] + jnp.dot(
                    act, w2b_ref[:, cols], preferred_element_type=F32)

        def pair(p, carry):
            pre = [project_in(2 * p), project_in(2 * p + 1)]
            acts = [activate(a) for a in pre]
            project_out(2 * p, acts[0])
            project_out(2 * p + 1, acts[1])
            return carry

        lax.fori_loop(0, nsub // 2, pair, 0)

        @pl.when(nsub % 2 == 1)
        def _():
            project_out(nsub - 1, activate(project_in(nsub - 1)))


def _moe(item_e, item_ns, item_blk, xs, w_in, b_in, w_out, b_out):
    r_pad, d = xs.shape
    n_items = r_pad // MOE_ROWS
    e, _, ff2 = w_in.shape
    nj = ff2 // (2 * MOE_FC)

    def chunk(i, j, ns):
        return jnp.where(ns[i] > 0, j, nj - 1)

    grid_spec = pltpu.PrefetchScalarGridSpec(
        num_scalar_prefetch=3,
        grid=(n_items, nj),
        in_specs=[
            pl.BlockSpec((MOE_ROWS, d), lambda i, j, ie, ns, blk: (blk[i], 0)),
            pl.BlockSpec((1, d, 2 * MOE_FC),
                         lambda i, j, ie, ns, blk: (ie[i], 0, chunk(i, j, ns))),
            pl.BlockSpec((1, 1, 2 * MOE_FC),
                         lambda i, j, ie, ns, blk: (ie[i], 0, chunk(i, j, ns))),
            pl.BlockSpec((1, MOE_FC, d),
                         lambda i, j, ie, ns, blk: (ie[i], chunk(i, j, ns), 0)),
            pl.BlockSpec((1, 1, d), lambda i, j, ie, ns, blk: (ie[i], 0, 0)),
        ],
        out_specs=pl.BlockSpec((MOE_ROWS, d), lambda i, j, ie, ns, blk: (blk[i], 0)),
        scratch_shapes=[pltpu.VMEM((d, 2 * MOE_FC), BF16),
                        pltpu.VMEM((MOE_FC, d), BF16)],
    )
    return pl.pallas_call(
        _moe_kernel,
        grid_spec=grid_spec,
        out_shape=jax.ShapeDtypeStruct((r_pad, d), F32),
        compiler_params=_params(("arbitrary", "arbitrary")),
    )(item_e, item_ns, item_blk, xs, w_in, b_in.reshape(e, 1, ff2), w_out,
      b_out.reshape(e, 1, d))


def _combine_kernel(dest_ref, x_ref, w_ref, g_ref, ys_hbm, o_ref, buf_ref, sem_ref):
    tm = x_ref.shape[0]
    i = pl.program_id(0)
    n = pl.num_programs(0)
    slot = i % 2

    def row_copy(tile, slot_, r, k):
        src_row = dest_ref[(tile * tm + r) * TOP_K + k]
        return pltpu.make_async_copy(ys_hbm.at[pl.ds(src_row, 1), :],
                                     buf_ref.at[slot_, k, pl.ds(r, 1), :],
                                     sem_ref.at[slot_, k])

    def fetch(tile, slot_):
        def body(r, carry):
            for k in range(TOP_K):
                row_copy(tile, slot_, r, k).start()
            return carry
        lax.fori_loop(0, tm, body, 0)

    @pl.when(i == 0)
    def _():
        fetch(0, 0)

    @pl.when(i + 1 < n)
    def _():
        fetch(i + 1, 1 - slot)

    for k in range(TOP_K):
        pltpu.make_async_copy(ys_hbm.at[pl.ds(0, tm), :], buf_ref.at[slot, k],
                              sem_ref.at[slot, k]).wait()

    acc = x_ref[...]
    w = w_ref[...]
    for k in range(TOP_K):
        acc = acc + w[:, k:k + 1] * buf_ref[slot, k]
    o_ref[...] = _rms(acc, g_ref[...])


def _combine(flat_dest, x2, ys, wts, g, tm):
    t, d = x2.shape
    grid_spec = pltpu.PrefetchScalarGridSpec(
        num_scalar_prefetch=1,
        grid=(t // tm,),
        in_specs=[pl.BlockSpec((tm, d), lambda i, dest: (i, 0)),
                  pl.BlockSpec((tm, LANES), lambda i, dest: (i, 0)),
                  pl.BlockSpec((1, d), lambda i, dest: (0, 0)),
                  pl.BlockSpec(memory_space=pl.ANY)],
        out_specs=pl.BlockSpec((tm, d), lambda i, dest: (i, 0)),
        scratch_shapes=[pltpu.VMEM((2, TOP_K, tm, d), F32),
                        pltpu.SemaphoreType.DMA((2, TOP_K))],
    )
    return pl.pallas_call(
        _combine_kernel,
        grid_spec=grid_spec,
        out_shape=jax.ShapeDtypeStruct((t, d), F32),
        compiler_params=_params(("arbitrary",)),
    )(flat_dest, x2, wts, g.reshape(1, d), ys)


def _routing_tables(idx, rank, n_items):
    experts = jnp.arange(N_EXPERTS, dtype=jnp.int32)
    counts = jnp.sum((idx.reshape(-1, 1) == experts[None, :]).astype(jnp.int32), axis=0)
    n_it = (counts + MOE_ROWS - 1) // MOE_ROWS
    it_end = jnp.cumsum(n_it)
    it_start = it_end - n_it
    total = it_end[-1]
    dest = it_start[idx] * MOE_ROWS + rank

    items = jnp.arange(n_items, dtype=jnp.int32)
    valid = items < total
    item_blk = jnp.where(valid, items, total - 1)
    item_e = jnp.sum((it_end[None, :] <= item_blk[:, None]).astype(jnp.int32), axis=1)
    rows_left = counts[item_e] - (item_blk - it_start[item_e]) * MOE_ROWS
    n_sub = jnp.clip((rows_left + MOE_SUB - 1) // MOE_SUB, 0, MOE_ROWS // MOE_SUB)
    item_ns = jnp.where(valid, n_sub, 0).astype(jnp.int32)
    return dest, item_e, item_ns, item_blk.astype(jnp.int32)


def kernel(x, mem, norm_mix, w_in, b_gate, sgu_norm, sgu_w, sgu_b, w_branch_a, w_branch_b,
           w_out, norm_xattn, norm_mem, xa_w_q, xa_w_kv, xa_w_o, norm_moe, router_w,
           router_b, moe_w_in, moe_b_in, moe_w_out, moe_b_out, norm_final):
    b, s, d = x.shape
    t = b * s
    depth = norm_mix.shape[0]
    assert depth == 1, "the final rmsnorm is fused into the last layer's combine"
    n_mem = mem.shape[1]
    xt = x.reshape(t, d)
    tm_big = min(1024, t)
    n_items = -(-(t * TOP_K) // MOE_ROWS) + N_EXPERTS

    for l in range(depth):
        h = _rmsnorm(xt, norm_mix[l], 512)
        proj = _inproj(h, w_in[l], tm_big, COL_BLOCK)
        oa = _sb_attention(proj, b, s)
        x1 = _mix(oa, proj, xt, b_gate[l], sgu_norm[l], sgu_w[l], sgu_b[l],
                  w_branch_a[l].astype(BF16), w_branch_b[l].astype(BF16),
                  w_out[l].astype(BF16), 256)

        kv = _memkv(mem.reshape(b * n_mem, d), norm_mem[l], xa_w_kv[l])
        kv = kv.reshape(b, n_mem, 2 * XA_WIDTH)
        wr = jnp.pad(router_w[l], ((0, 0), (0, LANES - N_EXPERTS)))
        br = jnp.pad(router_b[l], (0, LANES - N_EXPERTS),
                     constant_values=-jnp.inf).reshape(1, LANES)
        x2, h3, idx, wts, rank = _xattn_router(
            x1, kv, norm_xattn[l], xa_w_q[l].astype(BF16), xa_w_o[l].astype(BF16),
            norm_moe[l], wr, br, b, 256)

        dest, item_e, item_ns, item_blk = _routing_tables(
            idx[:, :TOP_K], rank[:, :TOP_K], n_items)
        flat_dest = dest.reshape(-1)
        r_pad = n_items * MOE_ROWS
        src_tok = jnp.zeros((r_pad,), jnp.int32).at[flat_dest].set(
            jnp.arange(t * TOP_K, dtype=jnp.int32) // TOP_K)
        xs = jnp.take(h3, src_tok, axis=0)
        ys = _moe(item_e, item_ns, item_blk, xs,
                  moe_w_in[l], moe_b_in[l], moe_w_out[l], moe_b_out[l])
        xt = _combine(flat_dest, x2, ys, wts, norm_final, 256)
    return xt.reshape(b, s, d)
```

```python
import functools

import jax
import jax.numpy as jnp
from jax import lax
from jax.experimental import pallas as pl
from jax.experimental.pallas import tpu as pltpu

SB_HEADS = 16
SB_HEAD_DIM = 64
SB_WIDTH = SB_HEADS * SB_HEAD_DIM
SGU_GROUPS = 8
SGU_CHUNK = 128
SGU_WIDTH = 1024
XA_HEADS = 4
XA_HEAD_DIM = 128
XA_WIDTH = XA_HEADS * XA_HEAD_DIM
N_EXPERTS = 32
TOP_K = 4
SWIGLU_ALPHA = 1.702
SWIGLU_LIMIT = 7.0
RMS_EPS = 1e-5

LANES = 128
MXU_DIM = 256
COL_BLOCK = 1024
VMEM_LIMIT = 56 * 1024 * 1024

ATT_TQ = 256
ATT_TK = 128

MOE_ROWS = 1280
MOE_SUB = 256
MOE_FC = 256

BF16 = jnp.bfloat16
F32 = jnp.float32


def _params(sem):
    return pltpu.CompilerParams(dimension_semantics=sem, vmem_limit_bytes=VMEM_LIMIT)


def _rms(x, g):
    r = lax.rsqrt(jnp.mean(x * x, axis=-1, keepdims=True) + RMS_EPS)
    return (x * r) * g


def _gelu(x):
    return 0.5 * x * (1.0 + lax.erf(x * (2.0 ** -0.5)))


def _rmsnorm_kernel(x_ref, g_ref, o_ref):
    o_ref[...] = _rms(x_ref[...], g_ref[...]).astype(o_ref.dtype)


def _rmsnorm(x, g, tm):
    t, d = x.shape
    return pl.pallas_call(
        _rmsnorm_kernel,
        grid=(t // tm,),
        in_specs=[pl.BlockSpec((tm, d), lambda i: (i, 0)),
                  pl.BlockSpec((1, d), lambda i: (0, 0))],
        out_specs=pl.BlockSpec((tm, d), lambda i: (i, 0)),
        out_shape=jax.ShapeDtypeStruct((t, d), BF16),
        compiler_params=_params(("parallel",)),
    )(x, g.reshape(1, d))


def _inproj_kernel(h_ref, w_ref, o_ref, wb_ref):
    @pl.when(pl.program_id(1) == 0)
    def _():
        wb_ref[...] = w_ref[...].astype(BF16)

    o_ref[...] = jnp.dot(h_ref[...], wb_ref[...],
                         preferred_element_type=F32).astype(o_ref.dtype)


def _inproj(h, w, tm, tn):
    t, d = h.shape
    n = w.shape[1]
    return pl.pallas_call(
        _inproj_kernel,
        grid=(n // tn, t // tm),
        in_specs=[pl.BlockSpec((tm, d), lambda j, i: (i, 0)),
                  pl.BlockSpec((d, tn), lambda j, i: (0, j))],
        out_specs=pl.BlockSpec((tm, tn), lambda j, i: (i, j)),
        out_shape=jax.ShapeDtypeStruct((t, n), BF16),
        scratch_shapes=[pltpu.VMEM((d, tn), BF16)],
        compiler_params=_params(("arbitrary", "arbitrary")),
    )(h, w)


def _sb_attn_kernel(q_ref, k_ref, v_ref, o_ref, c_ref, acc_ref):
    tq, tk = ATT_TQ, ATT_TK
    per = tq // tk
    qi = pl.program_id(2)
    lane = lax.broadcasted_iota(jnp.int32, (tq, LANES), 1)
    row = lax.broadcasted_iota(jnp.int32, (tq, tq), 0)
    col = lax.broadcasted_iota(jnp.int32, (tq, tq), 1)
    causal = col < row
    jj = lax.broadcasted_iota(jnp.int32, (2 * tk, 2 * tk), 0)
    ss = lax.broadcasted_iota(jnp.int32, (2 * tk, 2 * tk), 1)
    j_in = jnp.where(jj >= tk, jj - tk, jj)
    sums = jnp.where((ss >= tk) | (j_in > ss), 1.0, 0.0).astype(BF16)
    scale = SB_HEAD_DIM ** -0.5
    q2 = q_ref[...]
    qh = []
    for h in range(2):
        in_half = (lane >= SB_HEAD_DIM * h) & (lane < SB_HEAD_DIM * (h + 1))
        qh.append(jnp.where(in_half, q2, jnp.zeros_like(q2)) * scale)
    c_ref[...] = jnp.zeros_like(c_ref)
    acc_ref[...] = jnp.zeros_like(acc_ref)

    def step(kb, masked):
        k0 = pl.multiple_of(kb * tq, tq)
        kblk = k_ref[pl.ds(k0, tq), :]
        vblk = v_ref[pl.ds(k0, tq), :]
        zs = [lax.dot_general(qh[h], kblk, (((1,), (1,)), ((), ())),
                              preferred_element_type=F32) for h in range(2)]
        log_beta, log_keep = [], []
        for z in zs:
            sp = jnp.log(1.0 + jnp.exp(-jnp.abs(z)))
            mz = jnp.minimum(z, 0.0)
            lk = (mz - z) - sp
            if masked:
                lk = jnp.where(causal, lk, 0.0)
            log_beta.append(mz - sp)
            log_keep.append(lk)
        sfx = {}
        for h in range(2):
            for r in reversed(range(per)):
                part = log_keep[h][:, r * tk:(r + 1) * tk]
                hi = part.astype(BF16)
                lo = (part - hi.astype(F32)).astype(BF16)
                sfx[h, r] = jnp.dot(jnp.concatenate([hi, lo], axis=1), sums,
                                    preferred_element_type=F32)
        for h in range(2):
            c = c_ref[h]
            parts = [None] * per
            for r in reversed(range(per)):
                parts[r] = jnp.exp(log_beta[h][:, r * tk:(r + 1) * tk] + c + sfx[h, r][:, :tk])
                c = c + sfx[h, r][:, tk:]
            c_ref[h] = c
            a = jnp.concatenate(parts, axis=1)
            if masked:
                a = jnp.where(causal, a, 0.0)
            acc_ref[h] = acc_ref[h] + jnp.dot(a.astype(BF16), vblk,
                                              preferred_element_type=F32)

    step(qi, True)

    def body(t, carry):
        step(qi - 1 - t, False)
        return carry

    lax.fori_loop(0, qi, body, 0)
    o_ref[...] = jnp.where(lane < SB_HEAD_DIM, acc_ref[0], acc_ref[1]).astype(o_ref.dtype)


def _sb_attention(proj, b, s):
    tq = ATT_TQ
    pairs = SB_WIDTH // LANES
    nq = s // tq
    kcol = COL_BLOCK // LANES
    return pl.pallas_call(
        _sb_attn_kernel,
        grid=(b, pairs, nq),
        in_specs=[
            pl.BlockSpec((tq, LANES), lambda bi, p, i: (bi * nq + i, p)),
            pl.BlockSpec((s, LANES), lambda bi, p, i: (bi, kcol + p)),
            pl.BlockSpec((s, LANES), lambda bi, p, i: (bi, 2 * kcol + p)),
        ],
        out_specs=pl.BlockSpec((tq, LANES), lambda bi, p, i: (bi * nq + i, p)),
        out_shape=jax.ShapeDtypeStruct((b * s, SB_WIDTH), BF16),
        scratch_shapes=[pltpu.VMEM((2, tq, ATT_TK), F32),
                        pltpu.VMEM((2, tq, LANES), F32)],
        compiler_params=_params(("parallel", "parallel", "arbitrary")),
    )(proj, proj, proj)


def _mix_kernel(oa_ref, u_ref, vs_ref, ga0_ref, ga1_ref, gb0_ref, gb1_ref, x_ref,
                bg_ref, sn_ref, sw_ref, sbt_ref, wba_ref, wbb_ref, wo_ref,
                o_ref, ob_ref):
    tm = x_ref.shape[0]
    p_i = lax.broadcasted_iota(jnp.int32, (SGU_CHUNK, SGU_CHUNK), 0)
    q_i = lax.broadcasted_iota(jnp.int32, (SGU_CHUNK, SGU_CHUNK), 1)
    tril = q_i <= p_i
    for g in range(SGU_GROUPS):
        lanes = slice(g * LANES, (g + 1) * LANES)
        wg = jnp.where(tril, sw_ref[g], 0.0).astype(BF16)
        gn = sn_ref[g:g + 1, :]
        bcol = sbt_ref[:, g:g + 1]
        for c in range(tm // SGU_CHUNK):
            rows = slice(c * SGU_CHUNK, (c + 1) * SGU_CHUNK)
            vn = _rms(_gelu(vs_ref[rows, lanes].astype(F32)), gn)
            mixed = jnp.dot(wg, vn.astype(BF16), preferred_element_type=F32) + bcol
            uu = _gelu(u_ref[rows, lanes].astype(F32))
            ob_ref[rows, lanes] = (uu * mixed).astype(BF16)

    ya = jnp.dot(oa_ref[...], wba_ref[...], preferred_element_type=F32)
    yb = jnp.dot(ob_ref[...], wbb_ref[...], preferred_element_type=F32)
    half = ya.shape[1] // 2
    pieces = []
    for k, (ga_ref, gb_ref) in enumerate(((ga0_ref, gb0_ref), (ga1_ref, gb1_ref))):
        cols = slice(k * half, (k + 1) * half)
        ga = jax.nn.sigmoid(ga_ref[...].astype(F32) + bg_ref[0:1, cols])
        gb = jax.nn.sigmoid(gb_ref[...].astype(F32) + bg_ref[1:2, cols])
        pieces.append((ga * ya[:, cols] + gb * yb[:, cols]).astype(BF16))
    merged = jnp.concatenate(pieces, axis=1)
    o_ref[...] = x_ref[...] + jnp.dot(merged, wo_ref[...], preferred_element_type=F32)


def _const_spec(shape):
    nd = len(shape)
    return pl.BlockSpec(shape, lambda i: (0,) * nd, pipeline_mode=pl.Buffered(1))


def _mix(oa, proj, x, b_gate, sgu_norm, sgu_w, sgu_b, wba, wbb, wo, tm):
    t, d = x.shape

    def col(cidx):
        return pl.BlockSpec((tm, COL_BLOCK), lambda i: (i, cidx))

    return pl.pallas_call(
        _mix_kernel,
        grid=(t // tm,),
        in_specs=[
            pl.BlockSpec((tm, SB_WIDTH), lambda i: (i, 0)),
            col(3), col(4), col(5), col(6), col(7), col(8),
            pl.BlockSpec((tm, d), lambda i: (i, 0)),
            _const_spec((2, d)),
            _const_spec((SGU_GROUPS, LANES)),
            _const_spec((SGU_GROUPS, SGU_CHUNK, SGU_CHUNK)),
            _const_spec((SGU_CHUNK, SGU_GROUPS)),
            _const_spec((SB_WIDTH, d)),
            _const_spec((SGU_WIDTH, d)),
            _const_spec((d, d)),
        ],
        out_specs=pl.BlockSpec((tm, d), lambda i: (i, 0)),
        out_shape=jax.ShapeDtypeStruct((t, d), F32),
        scratch_shapes=[pltpu.VMEM((tm, SGU_WIDTH), BF16)],
        compiler_params=_params(("parallel",)),
    )(oa, proj, proj, proj, proj, proj, proj, x,
      b_gate.reshape(2, d), sgu_norm, sgu_w, sgu_b.T, wba, wbb, wo)


def _memkv_kernel(m_ref, g_ref, w_ref, o_ref):
    h = _rms(m_ref[...], g_ref[...]).astype(BF16)
    o_ref[...] = jnp.dot(h, w_ref[...].astype(BF16),
                         preferred_element_type=F32).astype(o_ref.dtype)


def _memkv(mem2d, g, w):
    m, d = mem2d.shape
    n = w.shape[1]
    tn = n // 2
    return pl.pallas_call(
        _memkv_kernel,
        grid=(n // tn,),
        in_specs=[pl.BlockSpec((m, d), lambda j: (0, 0)),
                  pl.BlockSpec((1, d), lambda j: (0, 0)),
                  pl.BlockSpec((d, tn), lambda j: (0, j))],
        out_specs=pl.BlockSpec((m, tn), lambda j: (0, j)),
        out_shape=jax.ShapeDtypeStruct((m, n), BF16),
        compiler_params=_params(("parallel",)),
    )(mem2d, g.reshape(1, d), w)


def _xattn_router_kernel(x1_ref, gx_ref, wq_ref, k_ref, v_ref, wo_ref, gm_ref,
                         wr_ref, br_ref, x2_ref, h3_ref, idx_ref, wt_ref, rank_ref,
                         cnt_ref):
    tm = x1_ref.shape[0]

    @pl.when(pl.program_id(0) == 0)
    def _():
        cnt_ref[...] = jnp.zeros_like(cnt_ref)

    x1 = x1_ref[...]
    hn = _rms(x1, gx_ref[...]).astype(BF16)
    q = jnp.dot(hn, wq_ref[...], preferred_element_type=F32)
    scale = XA_HEAD_DIM ** -0.5
    heads = []
    for h in range(XA_HEADS):
        cols = slice(h * XA_HEAD_DIM, (h + 1) * XA_HEAD_DIM)
        s = lax.dot_general(q[:, cols].astype(BF16), k_ref[0, :, cols],
                            (((1,), (1,)), ((), ())), preferred_element_type=F32) * scale
        e = jnp.exp(s - jnp.max(s, axis=-1, keepdims=True))
        p = e / jnp.sum(e, axis=-1, keepdims=True)
        heads.append(jnp.dot(p.astype(BF16), v_ref[0, :, cols],
                             preferred_element_type=F32).astype(BF16))
    o = jnp.concatenate(heads, axis=1)
    x2 = x1 + jnp.dot(o, wo_ref[...], preferred_element_type=F32)
    x2_ref[...] = x2

    h3 = _rms(x2, gm_ref[...])
    h3_ref[...] = h3
    logits = jnp.dot(h3, wr_ref[...], preferred_element_type=F32,
                     precision=lax.Precision.HIGHEST) + br_ref[...]

    lane = lax.broadcasted_iota(jnp.int32, (tm, LANES), 1)
    lane_f = lane.astype(F32)
    neg_inf = jnp.float32(-jnp.inf)
    work = logits
    hots, vals, ids = [], [], []
    for _ in range(TOP_K):
        m = jnp.max(work, axis=-1, keepdims=True)
        first = jnp.min(jnp.where(work == m, lane_f, float(LANES)), axis=-1, keepdims=True)
        hot = lane_f == first
        work = jnp.where(hot, neg_inf, work)
        hots.append(hot)
        vals.append(m)
        ids.append(first)

    exps = [jnp.exp(v - vals[0]) for v in vals]
    denom = exps[0] + exps[1] + exps[2] + exps[3]

    sel = jnp.zeros((tm, LANES), F32)
    for hot in hots:
        sel = sel + jnp.where(hot, 1.0, 0.0)
    r_i = lax.broadcasted_iota(jnp.int32, (tm, tm), 0)
    c_i = lax.broadcasted_iota(jnp.int32, (tm, tm), 1)
    before = (c_i < r_i).astype(BF16)
    seen = jnp.dot(before, sel.astype(BF16), preferred_element_type=F32) + cnt_ref[...]
    cnt_ref[...] = cnt_ref[...] + jnp.sum(sel, axis=0, keepdims=True)

    idx_out = jnp.zeros((tm, LANES), F32)
    wt_out = jnp.zeros((tm, LANES), F32)
    rank_out = jnp.zeros((tm, LANES), F32)
    for k in range(TOP_K):
        rank_k = jnp.sum(jnp.where(hots[k], seen, 0.0), axis=-1, keepdims=True)
        idx_out = jnp.where(lane == k, ids[k], idx_out)
        wt_out = jnp.where(lane == k, exps[k] / denom, wt_out)
        rank_out = jnp.where(lane == k, rank_k, rank_out)
    idx_ref[...] = idx_out.astype(jnp.int32)
    wt_ref[...] = wt_out
    rank_ref[...] = rank_out.astype(jnp.int32)


def _xattn_router(x1, kv, gx, wq, wo, gm, wr, br, b, tm):
    t, d = x1.shape
    per_b = t // b // tm
    n_mem = kv.shape[1]
    row = lambda i: (i, 0)
    return pl.pallas_call(
        _xattn_router_kernel,
        grid=(t // tm,),
        in_specs=[
            pl.BlockSpec((tm, d), row),
            _const_spec((1, d)),
            _const_spec((d, XA_WIDTH)),
            pl.BlockSpec((1, n_mem, XA_WIDTH), lambda i: (i // per_b, 0, 0)),
            pl.BlockSpec((1, n_mem, XA_WIDTH), lambda i: (i // per_b, 0, 1)),
            _const_spec((XA_WIDTH, d)),
            _const_spec((1, d)),
            _const_spec((d, LANES)),
            _const_spec((1, LANES)),
        ],
        out_specs=[
            pl.BlockSpec((tm, d), row),
            pl.BlockSpec((tm, d), row),
            pl.BlockSpec((tm, LANES), row),
            pl.BlockSpec((tm, LANES), row),
            pl.BlockSpec((tm, LANES), row),
        ],
        out_shape=[
            jax.ShapeDtypeStruct((t, d), F32),
            jax.ShapeDtypeStruct((t, d), F32),
            jax.ShapeDtypeStruct((t, LANES), jnp.int32),
            jax.ShapeDtypeStruct((t, LANES), F32),
            jax.ShapeDtypeStruct((t, LANES), jnp.int32),
        ],
        scratch_shapes=[pltpu.VMEM((1, LANES), F32)],
        compiler_params=_params(("arbitrary",)),
    )(x1, gx.reshape(1, d), wq, kv, kv, wo, gm.reshape(1, d), wr, br)


def _sort_rows_kernel(src_ref, valid_ref, h_hbm, o_ref, buf_ref, sem_ref):
    tm = o_ref.shape[0]
    i = pl.program_id(0)
    n = pl.num_programs(0)
    slot = i % 2

    def fetch(tile, slot_):
        def body(r, carry):
            tok = src_ref[tile * tm + r]
            pltpu.make_async_copy(h_hbm.at[pl.ds(tok, 1), :],
                                  buf_ref.at[slot_, pl.ds(r, 1), :],
                                  sem_ref.at[slot_]).start()
            return carry
        lax.fori_loop(0, tm, body, 0)

    @pl.when((i == 0) & (valid_ref[0] > 0))
    def _():
        fetch(0, 0)

    @pl.when((i + 1 < n) & (valid_ref[jnp.minimum(i + 1, n - 1)] > 0))
    def _():
        fetch(i + 1, 1 - slot)

    @pl.when(valid_ref[i] > 0)
    def _():
        pltpu.make_async_copy(h_hbm.at[pl.ds(0, tm), :], buf_ref.at[slot],
                              sem_ref.at[slot]).wait()
        o_ref[...] = buf_ref[slot].astype(o_ref.dtype)

    @pl.when(valid_ref[i] == 0)
    def _():
        o_ref[...] = jnp.zeros_like(o_ref)


def _sort_rows(src_tok, tile_valid, h3, tm):
    r_pad = src_tok.shape[0]
    d = h3.shape[1]
    grid_spec = pltpu.PrefetchScalarGridSpec(
        num_scalar_prefetch=2,
        grid=(r_pad // tm,),
        in_specs=[pl.BlockSpec(memory_space=pl.ANY)],
        out_specs=pl.BlockSpec((tm, d), lambda i, src, valid: (i, 0)),
        scratch_shapes=[pltpu.VMEM((2, tm, d), F32),
                        pltpu.SemaphoreType.DMA((2,))],
    )
    return pl.pallas_call(
        _sort_rows_kernel,
        grid_spec=grid_spec,
        out_shape=jax.ShapeDtypeStruct((r_pad, d), BF16),
        compiler_params=_params(("arbitrary",)),
    )(src_tok, tile_valid, h3)


def _swish_clamped(v):
    glu = jnp.minimum(v, SWIGLU_LIMIT)
    return glu * jax.nn.sigmoid(SWIGLU_ALPHA * glu)


def _lin_clamped(v):
    return jnp.clip(v, -SWIGLU_LIMIT, SWIGLU_LIMIT) + 1.0


def _moe_kernel(ie_ref, ns_ref, blk_ref, x_ref, w1_ref, b1_ref, w2_ref, b2_ref,
                o_ref, w1b_ref, w2b_ref):
    del ie_ref, blk_ref
    i = pl.program_id(0)
    j = pl.program_id(1)
    nsub = ns_ref[i]
    fc = w2_ref.shape[1]
    half = fc // 2
    d = o_ref.shape[1]

    @pl.when(nsub > 0)
    def _():
        w1b_ref[...] = w1_ref[0].astype(BF16)
        r_i = lax.broadcasted_iota(jnp.int32, (fc, fc), 0)
        c_i = lax.broadcasted_iota(jnp.int32, (fc, fc), 1)
        perm = jnp.where(c_i == (r_i >> 1) + half * (r_i & 1), 1.0, 0.0).astype(BF16)
        w2b_ref[...] = jnp.dot(perm, w2_ref[0].astype(BF16),
                               preferred_element_type=F32).astype(BF16)

        @pl.when(j == 0)
        def _():
            o_ref[...] = jnp.broadcast_to(b2_ref[0], o_ref.shape)

        b1 = b1_ref[0]
        lane = lax.broadcasted_iota(jnp.int32, (MOE_SUB, fc), 1)
        even = (lane & 1) == 0

        def sub_rows(s):
            return pl.ds(pl.multiple_of(s * MOE_SUB, MOE_SUB), MOE_SUB)

        def project_in(s):
            return jnp.dot(x_ref[sub_rows(s), :], w1b_ref[...],
                           preferred_element_type=F32) + b1

        def activate(a):
            a_a = a[:, :fc]
            a_b = a[:, fc:]
            p_a = _swish_clamped(a_a) * pltpu.roll(_lin_clamped(a_a), fc - 1, 1)
            p_b = pltpu.roll(_swish_clamped(a_b), 1, 1) * _lin_clamped(a_b)
            return jnp.where(even, p_a, p_b).astype(BF16)

        def project_out(s, act):
            rows = sub_rows(s)
            for n in range(d // MXU_DIM):
                cols = slice(n * MXU_DIM, (n + 1) * MXU_DIM)
                o_ref[rows, cols] = o_ref[rows, cols] + jnp.dot(
                    act, w2b_ref[:, cols], preferred_element_type=F32)

        def pair(p, carry):
            pre = [project_in(2 * p), project_in(2 * p + 1)]
            acts = [activate(a) for a in pre]
            project_out(2 * p, acts[0])
            project_out(2 * p + 1, acts[1])
            return carry

        lax.fori_loop(0, nsub // 2, pair, 0)

        @pl.when(nsub % 2 == 1)
        def _():
            project_out(nsub - 1, activate(project_in(nsub - 1)))


def _moe(item_e, item_ns, item_blk, xs, w_in, b_in, w_out, b_out):
    r_pad, d = xs.shape
    n_items = r_pad // MOE_ROWS
    e, _, ff2 = w_in.shape
    nj = ff2 // (2 * MOE_FC)

    def chunk(i, j, ns):
        return jnp.where(ns[i] > 0, j, nj - 1)

    grid_spec = pltpu.PrefetchScalarGridSpec(
        num_scalar_prefetch=3,
        grid=(n_items, nj),
        in_specs=[
            pl.BlockSpec((MOE_ROWS, d), lambda i, j, ie, ns, blk: (blk[i], 0)),
            pl.BlockSpec((1, d, 2 * MOE_FC),
                         lambda i, j, ie, ns, blk: (ie[i], 0, chunk(i, j, ns))),
            pl.BlockSpec((1, 1, 2 * MOE_FC),
                         lambda i, j, ie, ns, blk: (ie[i], 0, chunk(i, j, ns))),
            pl.BlockSpec((1, MOE_FC, d),
                         lambda i, j, ie, ns, blk: (ie[i], chunk(i, j, ns), 0)),
            pl.BlockSpec((1, 1, d), lambda i, j, ie, ns, blk: (ie[i], 0, 0)),
        ],
        out_specs=pl.BlockSpec((MOE_ROWS, d), lambda i, j, ie, ns, blk: (blk[i], 0)),
        scratch_shapes=[pltpu.VMEM((d, 2 * MOE_FC), BF16),
                        pltpu.VMEM((MOE_FC, d), BF16)],
    )
    return pl.pallas_call(
        _moe_kernel,
        grid_spec=grid_spec,
        out_shape=jax.ShapeDtypeStruct((r_pad, d), F32),
        compiler_params=_params(("arbitrary", "arbitrary")),
    )(item_e, item_ns, item_blk, xs, w_in, b_in.reshape(e, 1, ff2), w_out,
      b_out.reshape(e, 1, d))


def _combine_kernel(dest_ref, x_ref, w_ref, g_ref, ys_hbm, o_ref, buf_ref, sem_ref):
    tm = x_ref.shape[0]
    i = pl.program_id(0)
    n = pl.num_programs(0)
    slot = i % 2

    def row_copy(tile, slot_, r, k):
        src_row = dest_ref[(tile * tm + r) * TOP_K + k]
        return pltpu.make_async_copy(ys_hbm.at[pl.ds(src_row, 1), :],
                                     buf_ref.at[slot_, k, pl.ds(r, 1), :],
                                     sem_ref.at[slot_, k])

    def fetch(tile, slot_):
        def body(r, carry):
            for k in range(TOP_K):
                row_copy(tile, slot_, r, k).start()
            return carry
        lax.fori_loop(0, tm, body, 0)

    @pl.when(i == 0)
    def _():
        fetch(0, 0)

    @pl.when(i + 1 < n)
    def _():
        fetch(i + 1, 1 - slot)

    for k in range(TOP_K):
        pltpu.make_async_copy(ys_hbm.at[pl.ds(0, tm), :], buf_ref.at[slot, k],
                              sem_ref.at[slot, k]).wait()

    acc = x_ref[...]
    w = w_ref[...]
    for k in range(TOP_K):
        acc = acc + w[:, k:k + 1] * buf_ref[slot, k]
    o_ref[...] = _rms(acc, g_ref[...])


def _combine(flat_dest, x2, ys, wts, g, tm):
    t, d = x2.shape
    grid_spec = pltpu.PrefetchScalarGridSpec(
        num_scalar_prefetch=1,
        grid=(t // tm,),
        in_specs=[pl.BlockSpec((tm, d), lambda i, dest: (i, 0)),
                  pl.BlockSpec((tm, LANES), lambda i, dest: (i, 0)),
                  pl.BlockSpec((1, d), lambda i, dest: (0, 0)),
                  pl.BlockSpec(memory_space=pl.ANY)],
        out_specs=pl.BlockSpec((tm, d), lambda i, dest: (i, 0)),
        scratch_shapes=[pltpu.VMEM((2, TOP_K, tm, d), F32),
                        pltpu.SemaphoreType.DMA((2, TOP_K))],
    )
    return pl.pallas_call(
        _combine_kernel,
        grid_spec=grid_spec,
        out_shape=jax.ShapeDtypeStruct((t, d), F32),
        compiler_params=_params(("arbitrary",)),
    )(flat_dest, x2, wts, g.reshape(1, d), ys)


def _routing_tables(idx, rank, n_items):
    experts = jnp.arange(N_EXPERTS, dtype=jnp.int32)
    counts = jnp.sum((idx.reshape(-1, 1) == experts[None, :]).astype(jnp.int32), axis=0)
    n_it = (counts + MOE_ROWS - 1) // MOE_ROWS
    it_end = jnp.cumsum(n_it)
    it_start = it_end - n_it
    total = it_end[-1]
    dest = it_start[idx] * MOE_ROWS + rank

    items = jnp.arange(n_items, dtype=jnp.int32)
    valid = items < total
    item_blk = jnp.where(valid, items, total - 1)
    item_e = jnp.sum((it_end[None, :] <= item_blk[:, None]).astype(jnp.int32), axis=1)
    rows_left = counts[item_e] - (item_blk - it_start[item_e]) * MOE_ROWS
    n_sub = jnp.clip((rows_left + MOE_SUB - 1) // MOE_SUB, 0, MOE_ROWS // MOE_SUB)
    item_ns = jnp.where(valid, n_sub, 0).astype(jnp.int32)
    return dest, item_e, item_ns, item_blk.astype(jnp.int32)


def kernel(x, mem, norm_mix, w_in, b_gate, sgu_norm, sgu_w, sgu_b, w_branch_a, w_branch_b,
           w_out, norm_xattn, norm_mem, xa_w_q, xa_w_kv, xa_w_o, norm_moe, router_w,
           router_b, moe_w_in, moe_b_in, moe_w_out, moe_b_out, norm_final):
    b, s, d = x.shape
    t = b * s
    depth = norm_mix.shape[0]
    assert depth == 1, "the final rmsnorm is fused into the last layer's combine"
    n_mem = mem.shape[1]
    xt = x.reshape(t, d)
    tm_big = min(1024, t)
    n_items = -(-(t * TOP_K) // MOE_ROWS) + N_EXPERTS

    for l in range(depth):
        h = _rmsnorm(xt, norm_mix[l], 512)
        proj = _inproj(h, w_in[l], tm_big, COL_BLOCK)
        oa = _sb_attention(proj, b, s)
        x1 = _mix(oa, proj, xt, b_gate[l], sgu_norm[l], sgu_w[l], sgu_b[l],
                  w_branch_a[l].astype(BF16), w_branch_b[l].astype(BF16),
                  w_out[l].astype(BF16), 256)

        kv = _memkv(mem.reshape(b * n_mem, d), norm_mem[l], xa_w_kv[l])
        kv = kv.reshape(b, n_mem, 2 * XA_WIDTH)
        wr = jnp.pad(router_w[l], ((0, 0), (0, LANES - N_EXPERTS)))
        br = jnp.pad(router_b[l], (0, LANES - N_EXPERTS),
                     constant_values=-jnp.inf).reshape(1, LANES)
        x2, h3, idx, wts, rank = _xattn_router(
            x1, kv, norm_xattn[l], xa_w_q[l].astype(BF16), xa_w_o[l].astype(BF16),
            norm_moe[l], wr, br, b, 256)

        dest, item_e, item_ns, item_blk = _routing_tables(
            idx[:, :TOP_K], rank[:, :TOP_K], n_items)
        flat_dest = dest.reshape(-1)
        r_pad = n_items * MOE_ROWS
        src_tok = jnp.zeros((r_pad,), jnp.int32).at[flat_dest].set(
            jnp.arange(t * TOP_K, dtype=jnp.int32) // TOP_K)
        per_item = MOE_ROWS // MOE_SUB
        tiles = jnp.arange(n_items * per_item, dtype=jnp.int32)
        tile_valid = ((tiles % per_item) < item_ns[tiles // per_item]).astype(jnp.int32)
        xs = _sort_rows(src_tok, tile_valid, h3, MOE_SUB)
        ys = _moe(item_e, item_ns, item_blk, xs,
                  moe_w_in[l], moe_b_in[l], moe_w_out[l], moe_b_out[l])
        xt = _combine(flat_dest, x2, ys, wts, norm_final, 256)
    return xt.reshape(b, s, d)
```

```python
import functools

import jax
import jax.numpy as jnp
from jax import lax
from jax.experimental import pallas as pl
from jax.experimental.pallas import tpu as pltpu

SB_HEADS = 16
SB_HEAD_DIM = 64
SB_WIDTH = SB_HEADS * SB_HEAD_DIM
SGU_GROUPS = 8
SGU_CHUNK = 128
SGU_WIDTH = 1024
XA_HEADS = 4
XA_HEAD_DIM = 128
XA_WIDTH = XA_HEADS * XA_HEAD_DIM
N_EXPERTS = 32
TOP_K = 4
SWIGLU_ALPHA = 1.702
SWIGLU_LIMIT = 7.0
RMS_EPS = 1e-5

LANES = 128
MXU_DIM = 256
COL_BLOCK = 1024
VMEM_LIMIT = 56 * 1024 * 1024

ATT_TQ = 256
ATT_TK = 128

MOE_ROWS = 1280
MOE_SUB = 256
MOE_FC = 256

BF16 = jnp.bfloat16
F32 = jnp.float32


def _params(sem):
    return pltpu.CompilerParams(dimension_semantics=sem, vmem_limit_bytes=VMEM_LIMIT)


def _rms(x, g):
    r = lax.rsqrt(jnp.mean(x * x, axis=-1, keepdims=True) + RMS_EPS)
    return (x * r) * g


def _gelu(x):
    return 0.5 * x * (1.0 + lax.erf(x * (2.0 ** -0.5)))


def _rmsnorm_kernel(x_ref, g_ref, o_ref):
    o_ref[...] = _rms(x_ref[...], g_ref[...]).astype(o_ref.dtype)


def _rmsnorm(x, g, tm):
    t, d = x.shape
    return pl.pallas_call(
        _rmsnorm_kernel,
        grid=(t // tm,),
        in_specs=[pl.BlockSpec((tm, d), lambda i: (i, 0)),
                  pl.BlockSpec((1, d), lambda i: (0, 0))],
        out_specs=pl.BlockSpec((tm, d), lambda i: (i, 0)),
        out_shape=jax.ShapeDtypeStruct((t, d), BF16),
        compiler_params=_params(("parallel",)),
    )(x, g.reshape(1, d))


def _inproj_kernel(h_ref, w_ref, o_ref, wb_ref):
    @pl.when(pl.program_id(1) == 0)
    def _():
        wb_ref[...] = w_ref[...].astype(BF16)

    o_ref[...] = jnp.dot(h_ref[...], wb_ref[...],
                         preferred_element_type=F32).astype(o_ref.dtype)


def _inproj(h, w, tm, tn):
    t, d = h.shape
    n = w.shape[1]
    return pl.pallas_call(
        _inproj_kernel,
        grid=(n // tn, t // tm),
        in_specs=[pl.BlockSpec((tm, d), lambda j, i: (i, 0)),
                  pl.BlockSpec((d, tn), lambda j, i: (0, j))],
        out_specs=pl.BlockSpec((tm, tn), lambda j, i: (i, j)),
        out_shape=jax.ShapeDtypeStruct((t, n), BF16),
        scratch_shapes=[pltpu.VMEM((d, tn), BF16)],
        compiler_params=_params(("arbitrary", "arbitrary")),
    )(h, w)


def _sb_attn_kernel(q_ref, k_ref, v_ref, o_ref, c_ref, acc_ref):
    tq, tk = ATT_TQ, ATT_TK
    per = tq // tk
    qi = pl.program_id(2)
    lane = lax.broadcasted_iota(jnp.int32, (tq, LANES), 1)
    row = lax.broadcasted_iota(jnp.int32, (tq, tq), 0)
    col = lax.broadcasted_iota(jnp.int32, (tq, tq), 1)
    causal = col < row
    jj = lax.broadcasted_iota(jnp.int32, (2 * tk, 2 * tk), 0)
    ss = lax.broadcasted_iota(jnp.int32, (2 * tk, 2 * tk), 1)
    j_in = jnp.where(jj >= tk, jj - tk, jj)
    sums = jnp.where((ss >= tk) | (j_in > ss), 1.0, 0.0).astype(BF16)
    scale = SB_HEAD_DIM ** -0.5
    q2 = q_ref[...]
    qh = []
    for h in range(2):
        in_half = (lane >= SB_HEAD_DIM * h) & (lane < SB_HEAD_DIM * (h + 1))
        qh.append(jnp.where(in_half, q2, jnp.zeros_like(q2)) * scale)
    c_ref[...] = jnp.zeros_like(c_ref)
    acc_ref[...] = jnp.zeros_like(acc_ref)

    def step(kbs, masked):
        kv = []
        for kb in kbs:
            k0 = pl.multiple_of(kb * tq, tq)
            kv.append((k_ref[pl.ds(k0, tq), :], v_ref[pl.ds(k0, tq), :]))
        units = [(h, b) for h in range(2) for b in range(len(kbs))]
        log_beta, log_keep, sfx = {}, {}, {}
        for h, b in units:
            z = lax.dot_general(qh[h], kv[b][0], (((1,), (1,)), ((), ())),
                                preferred_element_type=F32)
            sp = jnp.log(1.0 + jnp.exp(-jnp.abs(z)))
            mz = jnp.minimum(z, 0.0)
            lk = (mz - z) - sp
            if masked:
                lk = jnp.where(causal, lk, 0.0)
            log_beta[h, b] = mz - sp
            log_keep[h, b] = lk
        for h, b in units:
            for r in reversed(range(per)):
                part = log_keep[h, b][:, r * tk:(r + 1) * tk]
                hi = part.astype(BF16)
                lo = (part - hi.astype(F32)).astype(BF16)
                sfx[h, b, r] = jnp.dot(jnp.concatenate([hi, lo], axis=1), sums,
                                       preferred_element_type=F32)
        for h in range(2):
            c = c_ref[h]
            acc = acc_ref[h]
            for b in range(len(kbs)):
                parts = [None] * per
                for r in reversed(range(per)):
                    parts[r] = jnp.exp(log_beta[h, b][:, r * tk:(r + 1) * tk] + c
                                       + sfx[h, b, r][:, :tk])
                    c = c + sfx[h, b, r][:, tk:]
                a = jnp.concatenate(parts, axis=1)
                if masked:
                    a = jnp.where(causal, a, 0.0)
                acc = acc + jnp.dot(a.astype(BF16), kv[b][1], preferred_element_type=F32)
            c_ref[h] = c
            acc_ref[h] = acc

    step([qi], True)

    def body(t, carry):
        step([qi - 1 - 2 * t, qi - 2 - 2 * t], False)
        return carry

    lax.fori_loop(0, qi // 2, body, 0)

    @pl.when(qi % 2 == 1)
    def _():
        step([0], False)
    o_ref[...] = jnp.where(lane < SB_HEAD_DIM, acc_ref[0], acc_ref[1]).astype(o_ref.dtype)


def _sb_attention(proj, b, s):
    tq = ATT_TQ
    pairs = SB_WIDTH // LANES
    nq = s // tq
    kcol = COL_BLOCK // LANES
    return pl.pallas_call(
        _sb_attn_kernel,
        grid=(b, pairs, nq),
        in_specs=[
            pl.BlockSpec((tq, LANES), lambda bi, p, i: (bi * nq + i, p)),
            pl.BlockSpec((s, LANES), lambda bi, p, i: (bi, kcol + p)),
            pl.BlockSpec((s, LANES), lambda bi, p, i: (bi, 2 * kcol + p)),
        ],
        out_specs=pl.BlockSpec((tq, LANES), lambda bi, p, i: (bi * nq + i, p)),
        out_shape=jax.ShapeDtypeStruct((b * s, SB_WIDTH), BF16),
        scratch_shapes=[pltpu.VMEM((2, tq, ATT_TK), F32),
                        pltpu.VMEM((2, tq, LANES), F32)],
        compiler_params=_params(("parallel", "parallel", "arbitrary")),
    )(proj, proj, proj)


def _mix_kernel(oa_ref, u_ref, vs_ref, ga0_ref, ga1_ref, gb0_ref, gb1_ref, x_ref,
                bg_ref, sn_ref, sw_ref, sbt_ref, wba_ref, wbb_ref, wo_ref,
                o_ref, ob_ref):
    tm = x_ref.shape[0]
    p_i = lax.broadcasted_iota(jnp.int32, (SGU_CHUNK, SGU_CHUNK), 0)
    q_i = lax.broadcasted_iota(jnp.int32, (SGU_CHUNK, SGU_CHUNK), 1)
    tril = q_i <= p_i
    for g in range(SGU_GROUPS):
        lanes = slice(g * LANES, (g + 1) * LANES)
        wg = jnp.where(tril, sw_ref[g], 0.0).astype(BF16)
        gn = sn_ref[g:g + 1, :]
        bcol = sbt_ref[:, g:g + 1]
        for c in range(tm // SGU_CHUNK):
            rows = slice(c * SGU_CHUNK, (c + 1) * SGU_CHUNK)
            vn = _rms(_gelu(vs_ref[rows, lanes].astype(F32)), gn)
            mixed = jnp.dot(wg, vn.astype(BF16), preferred_element_type=F32) + bcol
            uu = _gelu(u_ref[rows, lanes].astype(F32))
            ob_ref[rows, lanes] = (uu * mixed).astype(BF16)

    ya = jnp.dot(oa_ref[...], wba_ref[...], preferred_element_type=F32)
    yb = jnp.dot(ob_ref[...], wbb_ref[...], preferred_element_type=F32)
    half = ya.shape[1] // 2
    pieces = []
    for k, (ga_ref, gb_ref) in enumerate(((ga0_ref, gb0_ref), (ga1_ref, gb1_ref))):
        cols = slice(k * half, (k + 1) * half)
        ga = jax.nn.sigmoid(ga_ref[...].astype(F32) + bg_ref[0:1, cols])
        gb = jax.nn.sigmoid(gb_ref[...].astype(F32) + bg_ref[1:2, cols])
        pieces.append((ga * ya[:, cols] + gb * yb[:, cols]).astype(BF16))
    merged = jnp.concatenate(pieces, axis=1)
    o_ref[...] = x_ref[...] + jnp.dot(merged, wo_ref[...], preferred_element_type=F32)


def _const_spec(shape):
    nd = len(shape)
    return pl.BlockSpec(shape, lambda i: (0,) * nd, pipeline_mode=pl.Buffered(1))


def _mix(oa, proj, x, b_gate, sgu_norm, sgu_w, sgu_b, wba, wbb, wo, tm):
    t, d = x.shape

    def col(cidx):
        return pl.BlockSpec((tm, COL_BLOCK), lambda i: (i, cidx))

    return pl.pallas_call(
        _mix_kernel,
        grid=(t // tm,),
        in_specs=[
            pl.BlockSpec((tm, SB_WIDTH), lambda i: (i, 0)),
            col(3), col(4), col(5), col(6), col(7), col(8),
            pl.BlockSpec((tm, d), lambda i: (i, 0)),
            _const_spec((2, d)),
            _const_spec((SGU_GROUPS, LANES)),
            _const_spec((SGU_GROUPS, SGU_CHUNK, SGU_CHUNK)),
            _const_spec((SGU_CHUNK, SGU_GROUPS)),
            _const_spec((SB_WIDTH, d)),
            _const_spec((SGU_WIDTH, d)),
            _const_spec((d, d)),
        ],
        out_specs=pl.BlockSpec((tm, d), lambda i: (i, 0)),
        out_shape=jax.ShapeDtypeStruct((t, d), F32),
        scratch_shapes=[pltpu.VMEM((tm, SGU_WIDTH), BF16)],
        compiler_params=_params(("parallel",)),
    )(oa, proj, proj, proj, proj, proj, proj, x,
      b_gate.reshape(2, d), sgu_norm, sgu_w, sgu_b.T, wba, wbb, wo)


def _memkv_kernel(m_ref, g_ref, w_ref, o_ref):
    h = _rms(m_ref[...], g_ref[...]).astype(BF16)
    o_ref[...] = jnp.dot(h, w_ref[...].astype(BF16),
                         preferred_element_type=F32).astype(o_ref.dtype)


def _memkv(mem2d, g, w):
    m, d = mem2d.shape
    n = w.shape[1]
    tn = n // 2
    return pl.pallas_call(
        _memkv_kernel,
        grid=(n // tn,),
        in_specs=[pl.BlockSpec((m, d), lambda j: (0, 0)),
                  pl.BlockSpec((1, d), lambda j: (0, 0)),
                  pl.BlockSpec((d, tn), lambda j: (0, j))],
        out_specs=pl.BlockSpec((m, tn), lambda j: (0, j)),
        out_shape=jax.ShapeDtypeStruct((m, n), BF16),
        compiler_params=_params(("parallel",)),
    )(mem2d, g.reshape(1, d), w)


def _xattn_router_kernel(x1_ref, gx_ref, wq_ref, k_ref, v_ref, wo_ref, gm_ref,
                         wr_ref, br_ref, x2_ref, h3_ref, idx_ref, wt_ref, rank_ref,
                         cnt_ref):
    tm = x1_ref.shape[0]

    @pl.when(pl.program_id(0) == 0)
    def _():
        cnt_ref[...] = jnp.zeros_like(cnt_ref)

    x1 = x1_ref[...]
    hn = _rms(x1, gx_ref[...]).astype(BF16)
    q = jnp.dot(hn, wq_ref[...], preferred_element_type=F32)
    scale = XA_HEAD_DIM ** -0.5
    heads = []
    for h in range(XA_HEADS):
        cols = slice(h * XA_HEAD_DIM, (h + 1) * XA_HEAD_DIM)
        s = lax.dot_general(q[:, cols].astype(BF16), k_ref[0, :, cols],
                            (((1,), (1,)), ((), ())), preferred_element_type=F32) * scale
        e = jnp.exp(s - jnp.max(s, axis=-1, keepdims=True))
        p = e / jnp.sum(e, axis=-1, keepdims=True)
        heads.append(jnp.dot(p.astype(BF16), v_ref[0, :, cols],
                             preferred_element_type=F32).astype(BF16))
    o = jnp.concatenate(heads, axis=1)
    x2 = x1 + jnp.dot(o, wo_ref[...], preferred_element_type=F32)
    x2_ref[...] = x2

    h3 = _rms(x2, gm_ref[...])
    h3_ref[...] = h3
    logits = jnp.dot(h3, wr_ref[...], preferred_element_type=F32,
                     precision=lax.Precision.HIGHEST) + br_ref[...]

    lane = lax.broadcasted_iota(jnp.int32, (tm, LANES), 1)
    lane_f = lane.astype(F32)
    neg_inf = jnp.float32(-jnp.inf)
    work = logits
    hots, vals, ids = [], [], []
    for _ in range(TOP_K):
        m = jnp.max(work, axis=-1, keepdims=True)
        first = jnp.min(jnp.where(work == m, lane_f, float(LANES)), axis=-1, keepdims=True)
        hot = lane_f == first
        work = jnp.where(hot, neg_inf, work)
        hots.append(hot)
        vals.append(m)
        ids.append(first)

    exps = [jnp.exp(v - vals[0]) for v in vals]
    denom = exps[0] + exps[1] + exps[2] + exps[3]

    sel = jnp.zeros((tm, LANES), F32)
    for hot in hots:
        sel = sel + jnp.where(hot, 1.0, 0.0)
    r_i = lax.broadcasted_iota(jnp.int32, (tm, tm), 0)
    c_i = lax.broadcasted_iota(jnp.int32, (tm, tm), 1)
    before = (c_i < r_i).astype(BF16)
    seen = jnp.dot(before, sel.astype(BF16), preferred_element_type=F32) + cnt_ref[...]
    cnt_ref[...] = cnt_ref[...] + jnp.sum(sel, axis=0, keepdims=True)

    idx_out = jnp.zeros((tm, LANES), F32)
    wt_out = jnp.zeros((tm, LANES), F32)
    rank_out = jnp.zeros((tm, LANES), F32)
    for k in range(TOP_K):
        rank_k = jnp.sum(jnp.where(hots[k], seen, 0.0), axis=-1, keepdims=True)
        idx_out = jnp.where(lane == k, ids[k], idx_out)
        wt_out = jnp.where(lane == k, exps[k] / denom, wt_out)
        rank_out = jnp.where(lane == k, rank_k, rank_out)
    idx_ref[...] = idx_out.astype(jnp.int32)
    wt_ref[...] = wt_out
    rank_ref[...] = rank_out.astype(jnp.int32)


def _xattn_router(x1, kv, gx, wq, wo, gm, wr, br, b, tm):
    t, d = x1.shape
    per_b = t // b // tm
    n_mem = kv.shape[1]
    row = lambda i: (i, 0)
    return pl.pallas_call(
        _xattn_router_kernel,
        grid=(t // tm,),
        in_specs=[
            pl.BlockSpec((tm, d), row),
            _const_spec((1, d)),
            _const_spec((d, XA_WIDTH)),
            pl.BlockSpec((1, n_mem, XA_WIDTH), lambda i: (i // per_b, 0, 0)),
            pl.BlockSpec((1, n_mem, XA_WIDTH), lambda i: (i // per_b, 0, 1)),
            _const_spec((XA_WIDTH, d)),
            _const_spec((1, d)),
            _const_spec((d, LANES)),
            _const_spec((1, LANES)),
        ],
        out_specs=[
            pl.BlockSpec((tm, d), row),
            pl.BlockSpec((tm, d), row),
            pl.BlockSpec((tm, LANES), row),
            pl.BlockSpec((tm, LANES), row),
            pl.BlockSpec((tm, LANES), row),
        ],
        out_shape=[
            jax.ShapeDtypeStruct((t, d), F32),
            jax.ShapeDtypeStruct((t, d), F32),
            jax.ShapeDtypeStruct((t, LANES), jnp.int32),
            jax.ShapeDtypeStruct((t, LANES), F32),
            jax.ShapeDtypeStruct((t, LANES), jnp.int32),
        ],
        scratch_shapes=[pltpu.VMEM((1, LANES), F32)],
        compiler_params=_params(("arbitrary",)),
    )(x1, gx.reshape(1, d), wq, kv, kv, wo, gm.reshape(1, d), wr, br)


def _sort_rows_kernel(src_ref, valid_ref, h_hbm, o_ref, buf_ref, sem_ref):
    tm = o_ref.shape[0]
    i = pl.program_id(0)
    n = pl.num_programs(0)
    slot = i % 2

    def fetch(tile, slot_):
        def body(r, carry):
            tok = src_ref[tile * tm + r]
            pltpu.make_async_copy(h_hbm.at[pl.ds(tok, 1), :],
                                  buf_ref.at[slot_, pl.ds(r, 1), :],
                                  sem_ref.at[slot_]).start()
            return carry
        lax.fori_loop(0, tm, body, 0)

    @pl.when((i == 0) & (valid_ref[0] > 0))
    def _():
        fetch(0, 0)

    @pl.when((i + 1 < n) & (valid_ref[jnp.minimum(i + 1, n - 1)] > 0))
    def _():
        fetch(i + 1, 1 - slot)

    @pl.when(valid_ref[i] > 0)
    def _():
        pltpu.make_async_copy(h_hbm.at[pl.ds(0, tm), :], buf_ref.at[slot],
                              sem_ref.at[slot]).wait()
        o_ref[...] = buf_ref[slot].astype(o_ref.dtype)

    @pl.when(valid_ref[i] == 0)
    def _():
        o_ref[...] = jnp.zeros_like(o_ref)


def _sort_rows(src_tok, tile_valid, h3, tm):
    r_pad = src_tok.shape[0]
    d = h3.shape[1]
    grid_spec = pltpu.PrefetchScalarGridSpec(
        num_scalar_prefetch=2,
        grid=(r_pad // tm,),
        in_specs=[pl.BlockSpec(memory_space=pl.ANY)],
        out_specs=pl.BlockSpec((tm, d), lambda i, src, valid: (i, 0)),
        scratch_shapes=[pltpu.VMEM((2, tm, d), F32),
                        pltpu.SemaphoreType.DMA((2,))],
    )
    return pl.pallas_call(
        _sort_rows_kernel,
        grid_spec=grid_spec,
        out_shape=jax.ShapeDtypeStruct((r_pad, d), BF16),
        compiler_params=_params(("arbitrary",)),
    )(src_tok, tile_valid, h3)


def _swish_clamped(v):
    glu = jnp.minimum(v, SWIGLU_LIMIT)
    return glu * jax.nn.sigmoid(SWIGLU_ALPHA * glu)


def _lin_clamped(v):
    return jnp.clip(v, -SWIGLU_LIMIT, SWIGLU_LIMIT) + 1.0


def _moe_kernel(ie_ref, ns_ref, blk_ref, x_ref, w1_ref, b1_ref, w2_ref, b2_ref,
                o_ref, w1b_ref, w2b_ref):
    del ie_ref, blk_ref
    i = pl.program_id(0)
    j = pl.program_id(1)
    nsub = ns_ref[i]
    fc = w2_ref.shape[1]
    half = fc // 2
    d = o_ref.shape[1]

    @pl.when(nsub > 0)
    def _():
        w1b_ref[...] = w1_ref[0].astype(BF16)
        r_i = lax.broadcasted_iota(jnp.int32, (fc, fc), 0)
        c_i = lax.broadcasted_iota(jnp.int32, (fc, fc), 1)
        perm = jnp.where(c_i == (r_i >> 1) + half * (r_i & 1), 1.0, 0.0).astype(BF16)
        w2b_ref[...] = jnp.dot(perm, w2_ref[0].astype(BF16),
                               preferred_element_type=F32).astype(BF16)

        @pl.when(j == 0)
        def _():
            o_ref[...] = jnp.broadcast_to(b2_ref[0], o_ref.shape)

        b1 = b1_ref[0]
        lane = lax.broadcasted_iota(jnp.int32, (MOE_SUB, fc), 1)
        even = (lane & 1) == 0

        def sub_rows(s):
            return pl.ds(pl.multiple_of(s * MOE_SUB, MOE_SUB), MOE_SUB)

        def project_in(s):
            return jnp.dot(x_ref[sub_rows(s), :], w1b_ref[...],
                           preferred_element_type=F32) + b1

        def activate(a):
            a_a = a[:, :fc]
            a_b = a[:, fc:]
            p_a = _swish_clamped(a_a) * pltpu.roll(_lin_clamped(a_a), fc - 1, 1)
            p_b = pltpu.roll(_swish_clamped(a_b), 1, 1) * _lin_clamped(a_b)
            return jnp.where(even, p_a, p_b).astype(BF16)

        def project_out(s, act):
            rows = sub_rows(s)
            for n in range(d // MXU_DIM):
                cols = slice(n * MXU_DIM, (n + 1) * MXU_DIM)
                o_ref[rows, cols] = o_ref[rows, cols] + jnp.dot(
                    act, w2b_ref[:, cols], preferred_element_type=F32)

        def pair(p, carry):
            pre = [project_in(2 * p), project_in(2 * p + 1)]
            acts = [activate(a) for a in pre]
            project_out(2 * p, acts[0])
            project_out(2 * p + 1, acts[1])
            return carry

        lax.fori_loop(0, nsub // 2, pair, 0)

        @pl.when(nsub % 2 == 1)
        def _():
            project_out(nsub - 1, activate(project_in(nsub - 1)))


def _moe(item_e, item_ns, item_blk, xs, w_in, b_in, w_out, b_out):
    r_pad, d = xs.shape
    n_items = r_pad // MOE_ROWS
    e, _, ff2 = w_in.shape
    nj = ff2 // (2 * MOE_FC)

    def chunk(i, j, ns):
        return jnp.where(ns[i] > 0, j, nj - 1)

    grid_spec = pltpu.PrefetchScalarGridSpec(
        num_scalar_prefetch=3,
        grid=(n_items, nj),
        in_specs=[
            pl.BlockSpec((MOE_ROWS, d), lambda i, j, ie, ns, blk: (blk[i], 0)),
            pl.BlockSpec((1, d, 2 * MOE_FC),
                         lambda i, j, ie, ns, blk: (ie[i], 0, chunk(i, j, ns))),
            pl.BlockSpec((1, 1, 2 * MOE_FC),
                         lambda i, j, ie, ns, blk: (ie[i], 0, chunk(i, j, ns))),
            pl.BlockSpec((1, MOE_FC, d),
                         lambda i, j, ie, ns, blk: (ie[i], chunk(i, j, ns), 0)),
            pl.BlockSpec((1, 1, d), lambda i, j, ie, ns, blk: (ie[i], 0, 0)),
        ],
        out_specs=pl.BlockSpec((MOE_ROWS, d), lambda i, j, ie, ns, blk: (blk[i], 0)),
        scratch_shapes=[pltpu.VMEM((d, 2 * MOE_FC), BF16),
                        pltpu.VMEM((MOE_FC, d), BF16)],
    )
    return pl.pallas_call(
        _moe_kernel,
        grid_spec=grid_spec,
        out_shape=jax.ShapeDtypeStruct((r_pad, d), F32),
        compiler_params=_params(("arbitrary", "arbitrary")),
    )(item_e, item_ns, item_blk, xs, w_in, b_in.reshape(e, 1, ff2), w_out,
      b_out.reshape(e, 1, d))


def _combine_kernel(dest_ref, x_ref, w_ref, g_ref, ys_hbm, o_ref, buf_ref, sem_ref):
    tm = x_ref.shape[0]
    i = pl.program_id(0)
    n = pl.num_programs(0)
    slot = i % 2

    def row_copy(tile, slot_, r, k):
        src_row = dest_ref[(tile * tm + r) * TOP_K + k]
        return pltpu.make_async_copy(ys_hbm.at[pl.ds(src_row, 1), :],
                                     buf_ref.at[slot_, k, pl.ds(r, 1), :],
                                     sem_ref.at[slot_, k])

    def fetch(tile, slot_):
        def body(r, carry):
            for k in range(TOP_K):
                row_copy(tile, slot_, r, k).start()
            return carry
        lax.fori_loop(0, tm, body, 0)

    @pl.when(i == 0)
    def _():
        fetch(0, 0)

    @pl.when(i + 1 < n)
    def _():
        fetch(i + 1, 1 - slot)

    for k in range(TOP_K):
        pltpu.make_async_copy(ys_hbm.at[pl.ds(0, tm), :], buf_ref.at[slot, k],
                              sem_ref.at[slot, k]).wait()

    acc = x_ref[...]
    w = w_ref[...]
    for k in range(TOP_K):
        acc = acc + w[:, k:k + 1] * buf_ref[slot, k]
    o_ref[...] = _rms(acc, g_ref[...])


def _combine(flat_dest, x2, ys, wts, g, tm):
    t, d = x2.shape
    grid_spec = pltpu.PrefetchScalarGridSpec(
        num_scalar_prefetch=1,
        grid=(t // tm,),
        in_specs=[pl.BlockSpec((tm, d), lambda i, dest: (i, 0)),
                  pl.BlockSpec((tm, LANES), lambda i, dest: (i, 0)),
                  pl.BlockSpec((1, d), lambda i, dest: (0, 0)),
                  pl.BlockSpec(memory_space=pl.ANY)],
        out_specs=pl.BlockSpec((tm, d), lambda i, dest: (i, 0)),
        scratch_shapes=[pltpu.VMEM((2, TOP_K, tm, d), F32),
                        pltpu.SemaphoreType.DMA((2, TOP_K))],
    )
    return pl.pallas_call(
        _combine_kernel,
        grid_spec=grid_spec,
        out_shape=jax.ShapeDtypeStruct((t, d), F32),
        compiler_params=_params(("arbitrary",)),
    )(flat_dest, x2, wts, g.reshape(1, d), ys)


def _routing_tables(idx, rank, n_items):
    experts = jnp.arange(N_EXPERTS, dtype=jnp.int32)
    counts = jnp.sum((idx.reshape(-1, 1) == experts[None, :]).astype(jnp.int32), axis=0)
    n_it = (counts + MOE_ROWS - 1) // MOE_ROWS
    it_end = jnp.cumsum(n_it)
    it_start = it_end - n_it
    total = it_end[-1]
    dest = it_start[idx] * MOE_ROWS + rank

    items = jnp.arange(n_items, dtype=jnp.int32)
    valid = items < total
    item_blk = jnp.where(valid, items, total - 1)
    item_e = jnp.sum((it_end[None, :] <= item_blk[:, None]).astype(jnp.int32), axis=1)
    rows_left = counts[item_e] - (item_blk - it_start[item_e]) * MOE_ROWS
    n_sub = jnp.clip((rows_left + MOE_SUB - 1) // MOE_SUB, 0, MOE_ROWS // MOE_SUB)
    item_ns = jnp.where(valid, n_sub, 0).astype(jnp.int32)
    return dest, item_e, item_ns, item_blk.astype(jnp.int32)


def kernel(x, mem, norm_mix, w_in, b_gate, sgu_norm, sgu_w, sgu_b, w_branch_a, w_branch_b,
           w_out, norm_xattn, norm_mem, xa_w_q, xa_w_kv, xa_w_o, norm_moe, router_w,
           router_b, moe_w_in, moe_b_in, moe_w_out, moe_b_out, norm_final):
    b, s, d = x.shape
    t = b * s
    depth = norm_mix.shape[0]
    assert depth == 1, "the final rmsnorm is fused into the last layer's combine"
    n_mem = mem.shape[1]
    xt = x.reshape(t, d)
    tm_big = min(1024, t)
    n_items = -(-(t * TOP_K) // MOE_ROWS) + N_EXPERTS

    for l in range(depth):
        h = _rmsnorm(xt, norm_mix[l], 512)
        proj = _inproj(h, w_in[l], tm_big, COL_BLOCK)
        oa = _sb_attention(proj, b, s)
        x1 = _mix(oa, proj, xt, b_gate[l], sgu_norm[l], sgu_w[l], sgu_b[l],
                  w_branch_a[l].astype(BF16), w_branch_b[l].astype(BF16),
                  w_out[l].astype(BF16), 256)

        kv = _memkv(mem.reshape(b * n_mem, d), norm_mem[l], xa_w_kv[l])
        kv = kv.reshape(b, n_mem, 2 * XA_WIDTH)
        wr = jnp.pad(router_w[l], ((0, 0), (0, LANES - N_EXPERTS)))
        br = jnp.pad(router_b[l], (0, LANES - N_EXPERTS),
                     constant_values=-jnp.inf).reshape(1, LANES)
        x2, h3, idx, wts, rank = _xattn_router(
            x1, kv, norm_xattn[l], xa_w_q[l].astype(BF16), xa_w_o[l].astype(BF16),
            norm_moe[l], wr, br, b, 256)

        dest, item_e, item_ns, item_blk = _routing_tables(
            idx[:, :TOP_K], rank[:, :TOP_K], n_items)
        flat_dest = dest.reshape(-1)
        r_pad = n_items * MOE_ROWS
        src_tok = jnp.zeros((r_pad,), jnp.int32).at[flat_dest].set(
            jnp.arange(t * TOP_K, dtype=jnp.int32) // TOP_K)
        per_item = MOE_ROWS // MOE_SUB
        tiles = jnp.arange(n_items * per_item, dtype=jnp.int32)
        tile_valid = ((tiles % per_item) < item_ns[tiles // per_item]).astype(jnp.int32)
        xs = _sort_rows(src_tok, tile_valid, h3, MOE_SUB)
        ys = _moe(item_e, item_ns, item_blk, xs,
                  moe_w_in[l], moe_b_in[l], moe_w_out[l], moe_b_out[l])
        xt = _combine(flat_dest, x2, ys, wts, norm_final, 256)
    return xt.reshape(b, s, d)
```

```python
import functools

import jax
import jax.numpy as jnp
from jax import lax
from jax.experimental import pallas as pl
from jax.experimental.pallas import tpu as pltpu

SB_HEADS = 16
SB_HEAD_DIM = 64
SB_WIDTH = SB_HEADS * SB_HEAD_DIM
SGU_GROUPS = 8
SGU_CHUNK = 128
SGU_WIDTH = 1024
XA_HEADS = 4
XA_HEAD_DIM = 128
XA_WIDTH = XA_HEADS * XA_HEAD_DIM
N_EXPERTS = 32
TOP_K = 4
SWIGLU_ALPHA = 1.702
SWIGLU_LIMIT = 7.0
RMS_EPS = 1e-5

LANES = 128
MXU_DIM = 256
COL_BLOCK = 1024
VMEM_LIMIT = 56 * 1024 * 1024

ATT_TQ = 256
ATT_TK = 128

MOE_ROWS = 1280
MOE_SUB = 256
MOE_FC = 256

BF16 = jnp.bfloat16
F32 = jnp.float32


def _params(sem):
    return pltpu.CompilerParams(dimension_semantics=sem, vmem_limit_bytes=VMEM_LIMIT)


def _rms(x, g):
    r = lax.rsqrt(jnp.mean(x * x, axis=-1, keepdims=True) + RMS_EPS)
    return (x * r) * g


def _gelu(x):
    return 0.5 * x * (1.0 + lax.erf(x * (2.0 ** -0.5)))


def _rmsnorm_kernel(x_ref, g_ref, o_ref):
    o_ref[...] = _rms(x_ref[...], g_ref[...]).astype(o_ref.dtype)


def _rmsnorm(x, g, tm):
    t, d = x.shape
    return pl.pallas_call(
        _rmsnorm_kernel,
        grid=(t // tm,),
        in_specs=[pl.BlockSpec((tm, d), lambda i: (i, 0)),
                  pl.BlockSpec((1, d), lambda i: (0, 0))],
        out_specs=pl.BlockSpec((tm, d), lambda i: (i, 0)),
        out_shape=jax.ShapeDtypeStruct((t, d), BF16),
        compiler_params=_params(("parallel",)),
    )(x, g.reshape(1, d))


def _inproj_kernel(h_ref, w_ref, o_ref, wb_ref):
    @pl.when(pl.program_id(1) == 0)
    def _():
        wb_ref[...] = w_ref[...].astype(BF16)

    o_ref[...] = jnp.dot(h_ref[...], wb_ref[...],
                         preferred_element_type=F32).astype(o_ref.dtype)


def _inproj(h, w, tm, tn):
    t, d = h.shape
    n = w.shape[1]
    return pl.pallas_call(
        _inproj_kernel,
        grid=(n // tn, t // tm),
        in_specs=[pl.BlockSpec((tm, d), lambda j, i: (i, 0)),
                  pl.BlockSpec((d, tn), lambda j, i: (0, j))],
        out_specs=pl.BlockSpec((tm, tn), lambda j, i: (i, j)),
        out_shape=jax.ShapeDtypeStruct((t, n), BF16),
        scratch_shapes=[pltpu.VMEM((d, tn), BF16)],
        compiler_params=_params(("arbitrary", "arbitrary")),
    )(h, w)


def _sb_attn_kernel(q_ref, k_ref, v_ref, o_ref, c_ref, acc_ref):
    tq, tk = ATT_TQ, ATT_TK
    per = tq // tk
    qi = pl.program_id(2)
    lane = lax.broadcasted_iota(jnp.int32, (tq, LANES), 1)
    row = lax.broadcasted_iota(jnp.int32, (tq, tq), 0)
    col = lax.broadcasted_iota(jnp.int32, (tq, tq), 1)
    causal = col < row
    jj = lax.broadcasted_iota(jnp.int32, (2 * tk, 2 * tk), 0)
    ss = lax.broadcasted_iota(jnp.int32, (2 * tk, 2 * tk), 1)
    j_in = jnp.where(jj >= tk, jj - tk, jj)
    sums = jnp.where((ss >= tk) | (j_in > ss), 1.0, 0.0).astype(BF16)
    scale = SB_HEAD_DIM ** -0.5
    q2 = q_ref[...]
    qh = []
    for h in range(2):
        in_half = (lane >= SB_HEAD_DIM * h) & (lane < SB_HEAD_DIM * (h + 1))
        qh.append(jnp.where(in_half, q2, jnp.zeros_like(q2)) * scale)
    c_ref[...] = jnp.zeros_like(c_ref)
    acc_ref[...] = jnp.zeros_like(acc_ref)

    def step(kbs, masked):
        kv = []
        for kb in kbs:
            k0 = pl.multiple_of(kb * tq, tq)
            kv.append((k_ref[pl.ds(k0, tq), :], v_ref[pl.ds(k0, tq), :]))
        units = [(h, b) for h in range(2) for b in range(len(kbs))]
        log_beta, log_keep, sfx = {}, {}, {}
        for h, b in units:
            z = lax.dot_general(qh[h], kv[b][0], (((1,), (1,)), ((), ())),
                                preferred_element_type=F32)
            sp = jnp.log(1.0 + jnp.exp(-jnp.abs(z)))
            mz = jnp.minimum(z, 0.0)
            lk = (mz - z) - sp
            if masked:
                lk = jnp.where(causal, lk, 0.0)
            log_beta[h, b] = mz - sp
            log_keep[h, b] = lk
        for h, b in units:
            for r in reversed(range(per)):
                part = log_keep[h, b][:, r * tk:(r + 1) * tk]
                hi = part.astype(BF16)
                lo = (part - hi.astype(F32)).astype(BF16)
                sfx[h, b, r] = jnp.dot(jnp.concatenate([hi, lo], axis=1), sums,
                                       preferred_element_type=F32)
        for h in range(2):
            c = c_ref[h]
            acc = acc_ref[h]
            for b in range(len(kbs)):
                parts = [None] * per
                for r in reversed(range(per)):
                    parts[r] = jnp.exp(log_beta[h, b][:, r * tk:(r + 1) * tk] + c
                                       + sfx[h, b, r][:, :tk])
                    c = c + sfx[h, b, r][:, tk:]
                a = jnp.concatenate(parts, axis=1)
                if masked:
                    a = jnp.where(causal, a, 0.0)
                acc = acc + jnp.dot(a.astype(BF16), kv[b][1], preferred_element_type=F32)
            c_ref[h] = c
            acc_ref[h] = acc

    step([qi], True)

    def body(t, carry):
        step([qi - 1 - 2 * t, qi - 2 - 2 * t], False)
        return carry

    lax.fori_loop(0, qi // 2, body, 0)

    @pl.when(qi % 2 == 1)
    def _():
        step([0], False)
    o_ref[...] = jnp.where(lane < SB_HEAD_DIM, acc_ref[0], acc_ref[1]).astype(o_ref.dtype)


def _sb_attention(proj, b, s):
    tq = ATT_TQ
    pairs = SB_WIDTH // LANES
    nq = s // tq
    kcol = COL_BLOCK // LANES
    return pl.pallas_call(
        _sb_attn_kernel,
        grid=(b, pairs, nq),
        in_specs=[
            pl.BlockSpec((tq, LANES), lambda bi, p, i: (bi * nq + i, p)),
            pl.BlockSpec((s, LANES), lambda bi, p, i: (bi, kcol + p)),
            pl.BlockSpec((s, LANES), lambda bi, p, i: (bi, 2 * kcol + p)),
        ],
        out_specs=pl.BlockSpec((tq, LANES), lambda bi, p, i: (bi * nq + i, p)),
        out_shape=jax.ShapeDtypeStruct((b * s, SB_WIDTH), BF16),
        scratch_shapes=[pltpu.VMEM((2, tq, ATT_TK), F32),
                        pltpu.VMEM((2, tq, LANES), F32)],
        compiler_params=_params(("parallel", "parallel", "arbitrary")),
    )(proj, proj, proj)


def _mix_kernel(oa_ref, u_ref, vs_ref, ga0_ref, ga1_ref, gb0_ref, gb1_ref, x_ref,
                bg_ref, sn_ref, sw_ref, sbt_ref, wba_ref, wbb_ref, wo_ref,
                o_ref, ob_ref):
    tm = x_ref.shape[0]
    p_i = lax.broadcasted_iota(jnp.int32, (SGU_CHUNK, SGU_CHUNK), 0)
    q_i = lax.broadcasted_iota(jnp.int32, (SGU_CHUNK, SGU_CHUNK), 1)
    tril = q_i <= p_i
    for g in range(SGU_GROUPS):
        lanes = slice(g * LANES, (g + 1) * LANES)
        wg = jnp.where(tril, sw_ref[g], 0.0).astype(BF16)
        gn = sn_ref[g:g + 1, :]
        bcol = sbt_ref[:, g:g + 1]
        for c in range(tm // SGU_CHUNK):
            rows = slice(c * SGU_CHUNK, (c + 1) * SGU_CHUNK)
            vn = _rms(_gelu(vs_ref[rows, lanes].astype(F32)), gn)
            mixed = jnp.dot(wg, vn.astype(BF16), preferred_element_type=F32) + bcol
            uu = _gelu(u_ref[rows, lanes].astype(F32))
            ob_ref[rows, lanes] = (uu * mixed).astype(BF16)

    ya = jnp.dot(oa_ref[...], wba_ref[...], preferred_element_type=F32)
    yb = jnp.dot(ob_ref[...], wbb_ref[...], preferred_element_type=F32)
    half = ya.shape[1] // 2
    pieces = []
    for k, (ga_ref, gb_ref) in enumerate(((ga0_ref, gb0_ref), (ga1_ref, gb1_ref))):
        cols = slice(k * half, (k + 1) * half)
        ga = jax.nn.sigmoid(ga_ref[...].astype(F32) + bg_ref[0:1, cols])
        gb = jax.nn.sigmoid(gb_ref[...].astype(F32) + bg_ref[1:2, cols])
        pieces.append((ga * ya[:, cols] + gb * yb[:, cols]).astype(BF16))
    merged = jnp.concatenate(pieces, axis=1)
    o_ref[...] = x_ref[...] + jnp.dot(merged, wo_ref[...], preferred_element_type=F32)


def _const_spec(shape):
    nd = len(shape)
    return pl.BlockSpec(shape, lambda i: (0,) * nd, pipeline_mode=pl.Buffered(1))


def _mix(oa, proj, x, b_gate, sgu_norm, sgu_w, sgu_b, wba, wbb, wo, tm):
    t, d = x.shape

    def col(cidx):
        return pl.BlockSpec((tm, COL_BLOCK), lambda i: (i, cidx))

    return pl.pallas_call(
        _mix_kernel,
        grid=(t // tm,),
        in_specs=[
            pl.BlockSpec((tm, SB_WIDTH), lambda i: (i, 0)),
            col(3), col(4), col(5), col(6), col(7), col(8),
            pl.BlockSpec((tm, d), lambda i: (i, 0)),
            _const_spec((2, d)),
            _const_spec((SGU_GROUPS, LANES)),
            _const_spec((SGU_GROUPS, SGU_CHUNK, SGU_CHUNK)),
            _const_spec((SGU_CHUNK, SGU_GROUPS)),
            _const_spec((SB_WIDTH, d)),
            _const_spec((SGU_WIDTH, d)),
            _const_spec((d, d)),
        ],
        out_specs=pl.BlockSpec((tm, d), lambda i: (i, 0)),
        out_shape=jax.ShapeDtypeStruct((t, d), F32),
        scratch_shapes=[pltpu.VMEM((tm, SGU_WIDTH), BF16)],
        compiler_params=_params(("parallel",)),
    )(oa, proj, proj, proj, proj, proj, proj, x,
      b_gate.reshape(2, d), sgu_norm, sgu_w, sgu_b.T, wba, wbb, wo)


def _memkv_kernel(m_ref, g_ref, w_ref, o_ref):
    h = _rms(m_ref[...], g_ref[...]).astype(BF16)
    o_ref[...] = jnp.dot(h, w_ref[...].astype(BF16),
                         preferred_element_type=F32).astype(o_ref.dtype)


def _memkv(mem2d, g, w):
    m, d = mem2d.shape
    n = w.shape[1]
    tn = n // 2
    return pl.pallas_call(
        _memkv_kernel,
        grid=(n // tn,),
        in_specs=[pl.BlockSpec((m, d), lambda j: (0, 0)),
                  pl.BlockSpec((1, d), lambda j: (0, 0)),
                  pl.BlockSpec((d, tn), lambda j: (0, j))],
        out_specs=pl.BlockSpec((m, tn), lambda j: (0, j)),
        out_shape=jax.ShapeDtypeStruct((m, n), BF16),
        compiler_params=_params(("parallel",)),
    )(mem2d, g.reshape(1, d), w)


def _xattn_router_kernel(x1_ref, gx_ref, wq_ref, k_ref, v_ref, wo_ref, gm_ref,
                         wr_ref, br_ref, x2_ref, h3_ref, idx_ref, wt_ref, rank_ref,
                         cnt_ref):
    tm = x1_ref.shape[0]

    @pl.when(pl.program_id(0) == 0)
    def _():
        cnt_ref[...] = jnp.zeros_like(cnt_ref)

    x1 = x1_ref[...]
    hn = _rms(x1, gx_ref[...]).astype(BF16)
    q = jnp.dot(hn, wq_ref[...], preferred_element_type=F32)
    scale = XA_HEAD_DIM ** -0.5
    heads = []
    for h in range(XA_HEADS):
        cols = slice(h * XA_HEAD_DIM, (h + 1) * XA_HEAD_DIM)
        s = lax.dot_general(q[:, cols].astype(BF16), k_ref[0, :, cols],
                            (((1,), (1,)), ((), ())), preferred_element_type=F32) * scale
        e = jnp.exp(s - jnp.max(s, axis=-1, keepdims=True))
        p = e / jnp.sum(e, axis=-1, keepdims=True)
        heads.append(jnp.dot(p.astype(BF16), v_ref[0, :, cols],
                             preferred_element_type=F32).astype(BF16))
    o = jnp.concatenate(heads, axis=1)
    x2 = x1 + jnp.dot(o, wo_ref[...], preferred_element_type=F32)
    x2_ref[...] = x2

    h3 = _rms(x2, gm_ref[...])
    h3_ref[...] = h3
    logits = jnp.dot(h3, wr_ref[...], preferred_element_type=F32,
                     precision=lax.Precision.HIGHEST) + br_ref[...]

    lane = lax.broadcasted_iota(jnp.int32, (tm, LANES), 1)
    lane_f = lane.astype(F32)
    neg_inf = jnp.float32(-jnp.inf)
    work = logits
    hots, vals, ids = [], [], []
    for _ in range(TOP_K):
        m = jnp.max(work, axis=-1, keepdims=True)
        first = jnp.min(jnp.where(work == m, lane_f, float(LANES)), axis=-1, keepdims=True)
        hot = lane_f == first
        work = jnp.where(hot, neg_inf, work)
        hots.append(hot)
        vals.append(m)
        ids.append(first)

    exps = [jnp.exp(v - vals[0]) for v in vals]
    denom = exps[0] + exps[1] + exps[2] + exps[3]

    sel = jnp.zeros((tm, LANES), F32)
    for hot in hots:
        sel = sel + jnp.where(hot, 1.0, 0.0)
    r_i = lax.broadcasted_iota(jnp.int32, (tm, tm), 0)
    c_i = lax.broadcasted_iota(jnp.int32, (tm, tm), 1)
    before = (c_i < r_i).astype(BF16)
    seen = jnp.dot(before, sel.astype(BF16), preferred_element_type=F32) + cnt_ref[...]
    cnt_ref[...] = cnt_ref[...] + jnp.sum(sel, axis=0, keepdims=True)

    idx_out = jnp.zeros((tm, LANES), F32)
    wt_out = jnp.zeros((tm, LANES), F32)
    rank_out = jnp.zeros((tm, LANES), F32)
    for k in range(TOP_K):
        rank_k = jnp.sum(jnp.where(hots[k], seen, 0.0), axis=-1, keepdims=True)
        idx_out = jnp.where(lane == k, ids[k], idx_out)
        wt_out = jnp.where(lane == k, exps[k] / denom, wt_out)
        rank_out = jnp.where(lane == k, rank_k, rank_out)
    idx_ref[...] = idx_out.astype(jnp.int32)
    wt_ref[...] = wt_out
    rank_ref[...] = rank_out.astype(jnp.int32)


def _xattn_router(x1, kv, gx, wq, wo, gm, wr, br, b, tm):
    t, d = x1.shape
    per_b = t // b // tm
    n_mem = kv.shape[1]
    row = lambda i: (i, 0)
    return pl.pallas_call(
        _xattn_router_kernel,
        grid=(t // tm,),
        in_specs=[
            pl.BlockSpec((tm, d), row),
            _const_spec((1, d)),
            _const_spec((d, XA_WIDTH)),
            pl.BlockSpec((1, n_mem, XA_WIDTH), lambda i: (i // per_b, 0, 0)),
            pl.BlockSpec((1, n_mem, XA_WIDTH), lambda i: (i // per_b, 0, 1)),
            _const_spec((XA_WIDTH, d)),
            _const_spec((1, d)),
            _const_spec((d, LANES)),
            _const_spec((1, LANES)),
        ],
        out_specs=[
            pl.BlockSpec((tm, d), row),
            pl.BlockSpec((tm, d), row),
            pl.BlockSpec((tm, LANES), row),
            pl.BlockSpec((tm, LANES), row),
            pl.BlockSpec((tm, LANES), row),
        ],
        out_shape=[
            jax.ShapeDtypeStruct((t, d), F32),
            jax.ShapeDtypeStruct((t, d), F32),
            jax.ShapeDtypeStruct((t, LANES), jnp.int32),
            jax.ShapeDtypeStruct((t, LANES), F32),
            jax.ShapeDtypeStruct((t, LANES), jnp.int32),
        ],
        scratch_shapes=[pltpu.VMEM((1, LANES), F32)],
        compiler_params=_params(("arbitrary",)),
    )(x1, gx.reshape(1, d), wq, kv, kv, wo, gm.reshape(1, d), wr, br)


def _sort_rows_kernel(src_ref, valid_ref, h_hbm, o_ref, buf_ref, sem_ref):
    tm = o_ref.shape[0]
    i = pl.program_id(0)
    n = pl.num_programs(0)
    slot = i % 2

    def fetch(tile, slot_):
        def body(r, carry):
            tok = src_ref[tile * tm + r]
            pltpu.make_async_copy(h_hbm.at[pl.ds(tok, 1), :],
                                  buf_ref.at[slot_, pl.ds(r, 1), :],
                                  sem_ref.at[slot_]).start()
            return carry
        lax.fori_loop(0, tm, body, 0, unroll=8)

    @pl.when((i == 0) & (valid_ref[0] > 0))
    def _():
        fetch(0, 0)

    @pl.when((i + 1 < n) & (valid_ref[jnp.minimum(i + 1, n - 1)] > 0))
    def _():
        fetch(i + 1, 1 - slot)

    @pl.when(valid_ref[i] > 0)
    def _():
        pltpu.make_async_copy(h_hbm.at[pl.ds(0, tm), :], buf_ref.at[slot],
                              sem_ref.at[slot]).wait()
        o_ref[...] = buf_ref[slot].astype(o_ref.dtype)

    @pl.when(valid_ref[i] == 0)
    def _():
        o_ref[...] = jnp.zeros_like(o_ref)


def _sort_rows(src_tok, tile_valid, h3, tm):
    r_pad = src_tok.shape[0]
    d = h3.shape[1]
    grid_spec = pltpu.PrefetchScalarGridSpec(
        num_scalar_prefetch=2,
        grid=(r_pad // tm,),
        in_specs=[pl.BlockSpec(memory_space=pl.ANY)],
        out_specs=pl.BlockSpec((tm, d), lambda i, src, valid: (i, 0)),
        scratch_shapes=[pltpu.VMEM((2, tm, d), F32),
                        pltpu.SemaphoreType.DMA((2,))],
    )
    return pl.pallas_call(
        _sort_rows_kernel,
        grid_spec=grid_spec,
        out_shape=jax.ShapeDtypeStruct((r_pad, d), BF16),
        compiler_params=_params(("arbitrary",)),
    )(src_tok, tile_valid, h3)


def _swish_clamped(v):
    glu = jnp.minimum(v, SWIGLU_LIMIT)
    return glu * jax.nn.sigmoid(SWIGLU_ALPHA * glu)


def _lin_clamped(v):
    return jnp.clip(v, -SWIGLU_LIMIT, SWIGLU_LIMIT) + 1.0


def _moe_kernel(ie_ref, ns_ref, blk_ref, x_ref, w1_ref, b1_ref, w2_ref, b2_ref,
                o_ref, w1b_ref, w2b_ref):
    del ie_ref, blk_ref
    i = pl.program_id(0)
    j = pl.program_id(1)
    nsub = ns_ref[i]
    fc = w2_ref.shape[1]
    half = fc // 2
    d = o_ref.shape[1]

    @pl.when(nsub > 0)
    def _():
        w1b_ref[...] = w1_ref[0].astype(BF16)
        r_i = lax.broadcasted_iota(jnp.int32, (fc, fc), 0)
        c_i = lax.broadcasted_iota(jnp.int32, (fc, fc), 1)
        perm = jnp.where(c_i == (r_i >> 1) + half * (r_i & 1), 1.0, 0.0).astype(BF16)
        w2b_ref[...] = jnp.dot(perm, w2_ref[0].astype(BF16),
                               preferred_element_type=F32).astype(BF16)

        @pl.when(j == 0)
        def _():
            o_ref[...] = jnp.broadcast_to(b2_ref[0], o_ref.shape)

        b1 = b1_ref[0]
        lane = lax.broadcasted_iota(jnp.int32, (MOE_SUB, fc), 1)
        even = (lane & 1) == 0

        def sub_rows(s):
            return pl.ds(pl.multiple_of(s * MOE_SUB, MOE_SUB), MOE_SUB)

        def project_in(s):
            return jnp.dot(x_ref[sub_rows(s), :], w1b_ref[...],
                           preferred_element_type=F32) + b1

        def activate(a):
            a_a = a[:, :fc]
            a_b = a[:, fc:]
            p_a = _swish_clamped(a_a) * pltpu.roll(_lin_clamped(a_a), fc - 1, 1)
            p_b = pltpu.roll(_swish_clamped(a_b), 1, 1) * _lin_clamped(a_b)
            return jnp.where(even, p_a, p_b).astype(BF16)

        def project_out(s, act):
            rows = sub_rows(s)
            for n in range(d // MXU_DIM):
                cols = slice(n * MXU_DIM, (n + 1) * MXU_DIM)
                o_ref[rows, cols] = o_ref[rows, cols] + jnp.dot(
                    act, w2b_ref[:, cols], preferred_element_type=F32)

        def pair(p, carry):
            pre = [project_in(2 * p), project_in(2 * p + 1)]
            acts = [activate(a) for a in pre]
            project_out(2 * p, acts[0])
            project_out(2 * p + 1, acts[1])
            return carry

        lax.fori_loop(0, nsub // 2, pair, 0)

        @pl.when(nsub % 2 == 1)
        def _():
            project_out(nsub - 1, activate(project_in(nsub - 1)))


def _moe(item_e, item_ns, item_blk, xs, w_in, b_in, w_out, b_out):
    r_pad, d = xs.shape
    n_items = r_pad // MOE_ROWS
    e, _, ff2 = w_in.shape
    nj = ff2 // (2 * MOE_FC)

    def chunk(i, j, ns):
        return jnp.where(ns[i] > 0, j, nj - 1)

    grid_spec = pltpu.PrefetchScalarGridSpec(
        num_scalar_prefetch=3,
        grid=(n_items, nj),
        in_specs=[
            pl.BlockSpec((MOE_ROWS, d), lambda i, j, ie, ns, blk: (blk[i], 0)),
            pl.BlockSpec((1, d, 2 * MOE_FC),
                         lambda i, j, ie, ns, blk: (ie[i], 0, chunk(i, j, ns))),
            pl.BlockSpec((1, 1, 2 * MOE_FC),
                         lambda i, j, ie, ns, blk: (ie[i], 0, chunk(i, j, ns))),
            pl.BlockSpec((1, MOE_FC, d),
                         lambda i, j, ie, ns, blk: (ie[i], chunk(i, j, ns), 0)),
            pl.BlockSpec((1, 1, d), lambda i, j, ie, ns, blk: (ie[i], 0, 0)),
        ],
        out_specs=pl.BlockSpec((MOE_ROWS, d), lambda i, j, ie, ns, blk: (blk[i], 0)),
        scratch_shapes=[pltpu.VMEM((d, 2 * MOE_FC), BF16),
                        pltpu.VMEM((MOE_FC, d), BF16)],
    )
    return pl.pallas_call(
        _moe_kernel,
        grid_spec=grid_spec,
        out_shape=jax.ShapeDtypeStruct((r_pad, d), F32),
        compiler_params=_params(("arbitrary", "arbitrary")),
    )(item_e, item_ns, item_blk, xs, w_in, b_in.reshape(e, 1, ff2), w_out,
      b_out.reshape(e, 1, d))


def _combine_kernel(dest_ref, x_ref, w_ref, g_ref, ys_hbm, o_ref, buf_ref, sem_ref):
    tm = x_ref.shape[0]
    i = pl.program_id(0)
    n = pl.num_programs(0)
    slot = i % 2

    def row_copy(tile, slot_, r, k):
        src_row = dest_ref[(tile * tm + r) * TOP_K + k]
        return pltpu.make_async_copy(ys_hbm.at[pl.ds(src_row, 1), :],
                                     buf_ref.at[slot_, k, pl.ds(r, 1), :],
                                     sem_ref.at[slot_, k])

    def fetch(tile, slot_):
        def body(r, carry):
            for k in range(TOP_K):
                row_copy(tile, slot_, r, k).start()
            return carry
        lax.fori_loop(0, tm, body, 0, unroll=4)

    @pl.when(i == 0)
    def _():
        fetch(0, 0)

    @pl.when(i + 1 < n)
    def _():
        fetch(i + 1, 1 - slot)

    for k in range(TOP_K):
        pltpu.make_async_copy(ys_hbm.at[pl.ds(0, tm), :], buf_ref.at[slot, k],
                              sem_ref.at[slot, k]).wait()

    acc = x_ref[...]
    w = w_ref[...]
    for k in range(TOP_K):
        acc = acc + w[:, k:k + 1] * buf_ref[slot, k]
    o_ref[...] = _rms(acc, g_ref[...])


def _combine(flat_dest, x2, ys, wts, g, tm):
    t, d = x2.shape
    grid_spec = pltpu.PrefetchScalarGridSpec(
        num_scalar_prefetch=1,
        grid=(t // tm,),
        in_specs=[pl.BlockSpec((tm, d), lambda i, dest: (i, 0)),
                  pl.BlockSpec((tm, LANES), lambda i, dest: (i, 0)),
                  pl.BlockSpec((1, d), lambda i, dest: (0, 0)),
                  pl.BlockSpec(memory_space=pl.ANY)],
        out_specs=pl.BlockSpec((tm, d), lambda i, dest: (i, 0)),
        scratch_shapes=[pltpu.VMEM((2, TOP_K, tm, d), F32),
                        pltpu.SemaphoreType.DMA((2, TOP_K))],
    )
    return pl.pallas_call(
        _combine_kernel,
        grid_spec=grid_spec,
        out_shape=jax.ShapeDtypeStruct((t, d), F32),
        compiler_params=_params(("arbitrary",)),
    )(flat_dest, x2, wts, g.reshape(1, d), ys)


def _routing_tables(idx, rank, n_items):
    experts = jnp.arange(N_EXPERTS, dtype=jnp.int32)
    counts = jnp.sum((idx.reshape(-1, 1) == experts[None, :]).astype(jnp.int32), axis=0)
    n_it = (counts + MOE_ROWS - 1) // MOE_ROWS
    it_end = jnp.cumsum(n_it)
    it_start = it_end - n_it
    total = it_end[-1]
    dest = it_start[idx] * MOE_ROWS + rank

    items = jnp.arange(n_items, dtype=jnp.int32)
    valid = items < total
    item_blk = jnp.where(valid, items, total - 1)
    item_e = jnp.sum((it_end[None, :] <= item_blk[:, None]).astype(jnp.int32), axis=1)
    rows_left = counts[item_e] - (item_blk - it_start[item_e]) * MOE_ROWS
    n_sub = jnp.clip((rows_left + MOE_SUB - 1) // MOE_SUB, 0, MOE_ROWS // MOE_SUB)
    item_ns = jnp.where(valid, n_sub, 0).astype(jnp.int32)
    return dest, item_e, item_ns, item_blk.astype(jnp.int32)


def kernel(x, mem, norm_mix, w_in, b_gate, sgu_norm, sgu_w, sgu_b, w_branch_a, w_branch_b,
           w_out, norm_xattn, norm_mem, xa_w_q, xa_w_kv, xa_w_o, norm_moe, router_w,
           router_b, moe_w_in, moe_b_in, moe_w_out, moe_b_out, norm_final):
    b, s, d = x.shape
    t = b * s
    depth = norm_mix.shape[0]
    assert depth == 1, "the final rmsnorm is fused into the last layer's combine"
    n_mem = mem.shape[1]
    xt = x.reshape(t, d)
    tm_big = min(1024, t)
    n_items = -(-(t * TOP_K) // MOE_ROWS) + N_EXPERTS

    for l in range(depth):
        h = _rmsnorm(xt, norm_mix[l], 512)
        proj = _inproj(h, w_in[l], tm_big, COL_BLOCK)
        oa = _sb_attention(proj, b, s)
        x1 = _mix(oa, proj, xt, b_gate[l], sgu_norm[l], sgu_w[l], sgu_b[l],
                  w_branch_a[l].astype(BF16), w_branch_b[l].astype(BF16),
                  w_out[l].astype(BF16), 256)

        kv = _memkv(mem.reshape(b * n_mem, d), norm_mem[l], xa_w_kv[l])
        kv = kv.reshape(b, n_mem, 2 * XA_WIDTH)
        wr = jnp.pad(router_w[l], ((0, 0), (0, LANES - N_EXPERTS)))
        br = jnp.pad(router_b[l], (0, LANES - N_EXPERTS),
                     constant_values=-jnp.inf).reshape(1, LANES)
        x2, h3, idx, wts, rank = _xattn_router(
            x1, kv, norm_xattn[l], xa_w_q[l].astype(BF16), xa_w_o[l].astype(BF16),
            norm_moe[l], wr, br, b, 256)

        dest, item_e, item_ns, item_blk = _routing_tables(
            idx[:, :TOP_K], rank[:, :TOP_K], n_items)
        flat_dest = dest.reshape(-1)
        r_pad = n_items * MOE_ROWS
        src_tok = jnp.zeros((r_pad,), jnp.int32).at[flat_dest].set(
            jnp.arange(t * TOP_K, dtype=jnp.int32) // TOP_K)
        per_item = MOE_ROWS // MOE_SUB
        tiles = jnp.arange(n_items * per_item, dtype=jnp.int32)
        tile_valid = ((tiles % per_item) < item_ns[tiles // per_item]).astype(jnp.int32)
        xs = _sort_rows(src_tok, tile_valid, h3, MOE_SUB)
        ys = _moe(item_e, item_ns, item_blk, xs,
                  moe_w_in[l], moe_b_in[l], moe_w_out[l], moe_b_out[l])
        xt = _combine(flat_dest, x2, ys, wts, norm_final, 256)
    return xt.reshape(b, s, d)
```
